```python
import math
import jax, jax.numpy as jnp
from jax import lax
import numpy as np

D_MODEL = 1024
BATCH = 4
SEQ = 4096
DEPTH = 2
DEC_BATCH = 128
DEC_SEQ = 4
PAST_LEN = 2048
PAGE_SIZE = 128

HEAD_DIM = 64
H_A = 6
H_B = 6
C_GROUPS = 4
C_GROUP_DIM = 64
W_A = H_A * HEAD_DIM
W_B = H_B * HEAD_DIM
W_C = C_GROUPS * C_GROUP_DIM
MIX_WIDTH = W_A + W_B + W_C
IN_WIDTH = 3 * W_A + 3 * W_B + 2 * W_C
MOBA_BLOCK = 256
MOBA_TOPK = 3
MOBA_QBLOCK = 64
SB_QBLOCK = 128
CHUNK = 128
D_FF = 4 * D_MODEL
DEEPNORM_ALPHA = (2 * DEPTH) ** 0.25
DEEPNORM_BETA = (8 * DEPTH) ** -0.25
LN_EPS = 1e-5

kernel_name = 'hymba_moba_stickbreak_gmlp_deepnorm_step'


def layer_norm(x, g, b):
    xf = x.astype(jnp.float32)
    mu = xf.mean(-1, keepdims=True)
    var = jnp.square(xf - mu).mean(-1, keepdims=True)
    return ((xf - mu) * lax.rsqrt(var + LN_EPS) * g + b).astype(x.dtype)


def alibi_slopes(n):
    def pow2(m):
        start = 2.0 ** (-8.0 / m)
        return [start ** (i + 1) for i in range(m)]
    p = 2 ** int(math.floor(math.log2(n)))
    s = pow2(p)
    if p < n:
        s = s + pow2(2 * p)[0::2][: n - p]
    return jnp.asarray(np.array(s, dtype=np.float32))


def moba_attention(q, k, v, q_pos, slopes, q_block):
    Bn, H, Q, D = q.shape
    L = k.shape[2]
    n_blk = -(-L // MOBA_BLOCK)
    pad = n_blk * MOBA_BLOCK - L
    k = jnp.pad(k, ((0, 0), (0, 0), (0, pad), (0, 0)))
    v = jnp.pad(v, ((0, 0), (0, 0), (0, pad), (0, 0)))
    kb = k.reshape(Bn, H, n_blk, MOBA_BLOCK, D)
    vb = v.reshape(Bn, H, n_blk, MOBA_BLOCK, D)
    k_mean = kb.astype(jnp.float32).mean(3)
    k_sel = min(MOBA_TOPK, n_blk)
    scale = HEAD_DIM ** -0.5
    blk_ids = jnp.arange(n_blk)
    offs = jnp.arange(MOBA_BLOCK)
    bi = jnp.arange(Bn)[:, None, None]
    hi = jnp.arange(H)[None, :, None]

    def attend(args):
        qc, pc = args
        qb = pc.shape[0]
        cur = pc // MOBA_BLOCK
        gate = jnp.einsum('bhqd,bhnd->bhqn', qc.astype(jnp.float32), k_mean)
        gate = jnp.where(blk_ids[None, :] < cur[:, None], gate, -jnp.inf)
        _, top = lax.top_k(gate, k_sel)
        sel_ok = top < cur[:, None]
        cur_b = jnp.broadcast_to(cur[:, None], (Bn, H, qb, 1))
        blocks = jnp.concatenate([top, cur_b], -1)
        ok = jnp.concatenate([sel_ok, jnp.ones(cur_b.shape, bool)], -1)
        ns = blocks.shape[-1]
        flat = blocks.reshape(Bn, H, qb * ns)
        kg = kb[bi, hi, flat].reshape(Bn, H, qb, ns, MOBA_BLOCK, D)
        vg = vb[bi, hi, flat].reshape(Bn, H, qb, ns, MOBA_BLOCK, D)
        kpos = blocks[..., None] * MOBA_BLOCK + offs
        s = jnp.einsum('bhqd,bhqnsd->bhqns', qc, kg).astype(jnp.float32) * scale
        dist = (pc[:, None, None] - kpos).astype(jnp.float32)
        s = s - slopes[:, None, None, None] * dist
        mask = ok[..., None] & (kpos <= pc[:, None, None])
        s = jnp.where(mask, s, -jnp.inf)
        p = jax.nn.softmax(s.reshape(Bn, H, qb, ns * MOBA_BLOCK), axis=-1)
        p = p.reshape(Bn, H, qb, ns, MOBA_BLOCK).astype(v.dtype)
        return jnp.einsum('bhqns,bhqnsd->bhqd', p, vg)

    nq = Q // q_block
    qs = q.reshape(Bn, H, nq, q_block, D).transpose(2, 0, 1, 3, 4)
    ps = q_pos.reshape(nq, q_block)
    out = lax.map(attend, (qs, ps))
    return out.transpose(1, 2, 0, 3, 4).reshape(Bn, H, Q, D)


def stick_breaking_attention(q, k, v, q_pos, q_block):
    Bn, H, Q, D = q.shape
    L = k.shape[2]
    kpos = jnp.arange(L)
    scale = HEAD_DIM ** -0.5

    def attend(args):
        qc, pc = args
        z = jnp.einsum('bhqd,bhld->bhql', qc, k).astype(jnp.float32) * scale
        before = kpos[None, :] < pc[:, None]
        log_keep = jnp.where(before, jax.nn.log_sigmoid(-z), 0.0)
        after = lax.cumsum(log_keep, axis=3, reverse=True) - log_keep
        a = jnp.where(before, jnp.exp(jax.nn.log_sigmoid(z) + after), 0.0)
        return jnp.einsum('bhql,bhld->bhqd', a.astype(v.dtype), v)

    nq = Q // q_block
    qs = q.reshape(Bn, H, nq, q_block, D).transpose(2, 0, 1, 3, 4)
    ps = q_pos.reshape(nq, q_block)
    out = lax.map(attend, (qs, ps))
    return out.transpose(1, 2, 0, 3, 4).reshape(Bn, H, Q, D)


def chunk_spatial_gating(u, v, w_s, b_s):
    Bn, T, G, gd = v.shape
    n = min(T, CHUNK)
    w = jnp.tril(w_s[:, :n, :n])
    vc = v.reshape(Bn, T // n, n, G, gd)
    mix = jnp.einsum('gts,bcsgd->bctgd', w, vc) + jnp.transpose(b_s[:, :n])[:, :, None]
    return u * mix.reshape(Bn, T, G, gd)


def gather_pages(cache_l, page_table):
    g = cache_l[page_table]
    return g.reshape(g.shape[0], g.shape[1] * g.shape[2], g.shape[3], g.shape[4])


def hybrid_layer(h, past, q_pos, moba_qb, sb_qb, slopes, w_in, w_out, w_s, b_s,
                 lnc_g, lnc_b, ln1_g, ln1_b, w_ff1, w_ff2, ln2_g, ln2_b):
    Bn, T, _ = h.shape
    z = h @ w_in
    sizes = (W_A, W_A, W_A, W_B, W_B, W_B, W_C, W_C)
    splits = [sum(sizes[: i + 1]) for i in range(len(sizes) - 1)]
    qa, ka, va, qb, kb, vb, uc, vc = jnp.split(z, splits, axis=-1)
    qa = qa.reshape(Bn, T, H_A, HEAD_DIM)
    ka = ka.reshape(Bn, T, H_A, HEAD_DIM)
    va = va.reshape(Bn, T, H_A, HEAD_DIM)
    qb = qb.reshape(Bn, T, H_B, HEAD_DIM)
    kb = kb.reshape(Bn, T, H_B, HEAD_DIM)
    vb = vb.reshape(Bn, T, H_B, HEAD_DIM)
    if past is None:
        fka, fva, fkb, fvb = ka, va, kb, vb
    else:
        pka, pva, pkb, pvb = past
        fka = jnp.concatenate([pka, ka], axis=1)
        fva = jnp.concatenate([pva, va], axis=1)
        fkb = jnp.concatenate([pkb, kb], axis=1)
        fvb = jnp.concatenate([pvb, vb], axis=1)
    bhld = lambda t: t.transpose(0, 2, 1, 3)
    o_a = moba_attention(bhld(qa), bhld(fka), bhld(fva), q_pos, slopes, moba_qb)
    o_b = stick_breaking_attention(bhld(qb), bhld(fkb), bhld(fvb), q_pos, sb_qb)
    uc = jax.nn.gelu(uc).reshape(Bn, T, C_GROUPS, C_GROUP_DIM)
    vc = layer_norm(jax.nn.gelu(vc).reshape(Bn, T, C_GROUPS, C_GROUP_DIM), lnc_g, lnc_b)
    o_c = chunk_spatial_gating(uc, vc, w_s, b_s)
    mixed = jnp.concatenate([
        o_a.transpose(0, 2, 1, 3).reshape(Bn, T, W_A),
        o_b.transpose(0, 2, 1, 3).reshape(Bn, T, W_B),
        o_c.reshape(Bn, T, W_C)], axis=-1)
    h = layer_norm(DEEPNORM_ALPHA * h + mixed @ w_out, ln1_g, ln1_b)
    f = jnp.square(jax.nn.relu(h @ w_ff1)) @ w_ff2
    h = layer_norm(DEEPNORM_ALPHA * h + f, ln2_g, ln2_b)
    return h, (ka, va, kb, vb, vc.reshape(Bn, T, W_C))


def setup_inputs(seed: int = 0) -> dict:
    key = jax.random.key(seed)
    ks = jax.random.split(key, 24)
    n_pages = PAST_LEN // PAGE_SIZE
    n_used = DEC_BATCH * n_pages
    n_pool = n_used + n_used // 4
    f32 = jnp.float32
    nrm = lambda k, shape, s: jax.random.normal(k, shape, f32) * s
    cshape = (DEPTH, n_pool, PAGE_SIZE, H_A, HEAD_DIM)
    cshape_b = (DEPTH, n_pool, PAGE_SIZE, H_B, HEAD_DIM)
    page_table = jax.random.permutation(ks[6], n_pool)[:n_used].astype(jnp.int32).reshape(DEC_BATCH, n_pages)
    w_in = nrm(ks[7], (DEPTH, D_MODEL, IN_WIDTH), D_MODEL ** -0.5)
    col = jnp.arange(IN_WIDTH)
    is_value = ((col >= 2 * W_A) & (col < 3 * W_A)) | ((col >= 3 * W_A + 2 * W_B) & (col < 3 * W_A + 3 * W_B))
    w_in = w_in * jnp.where(is_value, DEEPNORM_BETA, 1.0).astype(f32)
    return {
        'x_prompt': nrm(ks[0], (BATCH, SEQ, D_MODEL), 1.0),
        'x_sample': nrm(ks[1], (DEC_BATCH, DEC_SEQ, D_MODEL), 1.0),
        'cache_k_a': nrm(ks[2], cshape, 1.0),
        'cache_v_a': nrm(ks[3], cshape, 1.0),
        'cache_k_b': nrm(ks[4], cshape_b, 1.0),
        'cache_v_b': nrm(ks[5], cshape_b, 1.0),
        'page_table': page_table,
        'ln_in_g': 1.0 + nrm(ks[8], (D_MODEL,), 0.02),
        'ln_in_b': nrm(ks[9], (D_MODEL,), 0.02),
        'w_in': w_in,
        'w_out': nrm(ks[10], (DEPTH, MIX_WIDTH, D_MODEL), MIX_WIDTH ** -0.5 * DEEPNORM_BETA),
        'w_spatial': nrm(ks[11], (DEPTH, C_GROUPS, CHUNK, CHUNK), CHUNK ** -0.5),
        'b_spatial': 1.0 + nrm(ks[12], (DEPTH, C_GROUPS, CHUNK), 0.1),
        'ln_c_g': 1.0 + nrm(ks[13], (DEPTH, C_GROUPS, C_GROUP_DIM), 0.02),
        'ln_c_b': nrm(ks[14], (DEPTH, C_GROUPS, C_GROUP_DIM), 0.02),
        'ln1_g': 1.0 + nrm(ks[15], (DEPTH, D_MODEL), 0.02),
        'ln1_b': nrm(ks[16], (DEPTH, D_MODEL), 0.02),
        'w_ff1': nrm(ks[17], (DEPTH, D_MODEL, D_FF), D_MODEL ** -0.5 * DEEPNORM_BETA),
        'w_ff2': nrm(ks[18], (DEPTH, D_FF, D_MODEL), D_FF ** -0.5 * DEEPNORM_BETA),
        'ln2_g': 1.0 + nrm(ks[19], (DEPTH, D_MODEL), 0.02),
        'ln2_b': nrm(ks[20], (DEPTH, D_MODEL), 0.02),
    }


def reference(x_prompt, x_sample, cache_k_a, cache_v_a, cache_k_b, cache_v_b, page_table,
              ln_in_g, ln_in_b, w_in, w_out, w_spatial, b_spatial, ln_c_g, ln_c_b,
              ln1_g, ln1_b, w_ff1, w_ff2, ln2_g, ln2_b):
    slopes = alibi_slopes(H_A)
    hp = layer_norm(x_prompt, ln_in_g, ln_in_b)
    hs = layer_norm(x_sample, ln_in_g, ln_in_b)
    pos_p = jnp.arange(SEQ, dtype=jnp.int32)
    pos_s = PAST_LEN + jnp.arange(DEC_SEQ, dtype=jnp.int32)
    st_p = []
    st_s = []
    for l in range(DEPTH):
        params = (w_in[l], w_out[l], w_spatial[l], b_spatial[l], ln_c_g[l], ln_c_b[l],
                  ln1_g[l], ln1_b[l], w_ff1[l], w_ff2[l], ln2_g[l], ln2_b[l])
        hp, sp = hybrid_layer(hp, None, pos_p, min(MOBA_QBLOCK, SEQ), min(SB_QBLOCK, SEQ), slopes, *params)
        past = (gather_pages(cache_k_a[l], page_table), gather_pages(cache_v_a[l], page_table),
                gather_pages(cache_k_b[l], page_table), gather_pages(cache_v_b[l], page_table))
        hs, ss = hybrid_layer(hs, past, pos_s, 1, DEC_SEQ, slopes, *params)
        st_p.append(sp)
        st_s.append(ss)
    k_a_prompt = jnp.stack([s[0] for s in st_p])
    v_a_prompt = jnp.stack([s[1] for s in st_p])
    k_b_prompt = jnp.stack([s[2] for s in st_p])
    v_b_prompt = jnp.stack([s[3] for s in st_p])
    k_a_sample = jnp.stack([s[0] for s in st_s])
    v_a_sample = jnp.stack([s[1] for s in st_s])
    k_b_sample = jnp.stack([s[2] for s in st_s])
    v_b_sample = jnp.stack([s[3] for s in st_s])
    v_c_sample = jnp.stack([s[4] for s in st_s])
    return (hp, hs, k_a_prompt, v_a_prompt, k_b_prompt, v_b_prompt,
            k_a_sample, v_a_sample, k_b_sample, v_b_sample, v_c_sample)
```

```python
import functools
import math

import jax
import jax.numpy as jnp
import numpy as np
from jax import lax
from jax.experimental import pallas as pl
from jax.experimental.pallas import tpu as pltpu

F32 = jnp.float32
BF16 = jnp.bfloat16

D_MODEL = 1024
HEAD_DIM = 64
N_HEADS = 6
W_ATT = N_HEADS * HEAD_DIM
C_GROUPS = 4
W_C = C_GROUPS * HEAD_DIM
IN_WIDTH = 6 * W_ATT + 2 * W_C
D_FF = 4 * D_MODEL
MOBA_BLOCK = 256
MOBA_TOPK = 3
CHUNK = 128
PAGE = 128
LN_EPS = 1e-5
LANES = 128
NEG_BIG = -1e30
SB_EXIT = -110.0

OFF_QA, OFF_KA, OFF_VA = 0, W_ATT, 2 * W_ATT
OFF_QB, OFF_KB, OFF_VB = 3 * W_ATT, 4 * W_ATT, 5 * W_ATT
OFF_UC, OFF_VC = 6 * W_ATT, 6 * W_ATT + W_C

VMEM_LIMIT = 56 * 1024 * 1024


def _alibi_slopes(n):
    def pow2(m):
        start = 2.0 ** (-8.0 / m)
        return [start ** (i + 1) for i in range(m)]
    p = 2 ** int(math.floor(math.log2(n)))
    s = pow2(p)
    if p < n:
        s = s + pow2(2 * p)[0::2][: n - p]
    return np.array(s, dtype=np.float32)


def _ln(x, g, b):
    mu = jnp.mean(x, axis=-1, keepdims=True)
    d = x - mu
    var = jnp.mean(d * d, axis=-1, keepdims=True)
    return d * lax.rsqrt(var + LN_EPS) * g + b


def _dot(a, b, precision=None):
    return jnp.dot(a, b, preferred_element_type=F32, precision=precision)


def _dot_nt(a, b, precision=None):
    return lax.dot_general(a, b, (((1,), (1,)), ((), ())),
                           preferred_element_type=F32, precision=precision)


def _params(sem):
    return pltpu.CompilerParams(dimension_semantics=sem, vmem_limit_bytes=VMEM_LIMIT)


def _in_proj_body(*refs, apply_ln, n_alias, with_kmean, tm):
    it = iter(refs)
    x_ref = next(it)
    if apply_ln:
        g_ref, b_ref = next(it), next(it)
    w_ref = next(it)
    for _ in range(n_alias):
        next(it)
    if apply_ln:
        h_ref = next(it)
    zb_ref = next(it)
    kv_refs = [next(it) for _ in range(4)]
    km_ref = next(it) if with_kmean else None

    x = x_ref[...]
    if apply_ln:
        x = _ln(x, g_ref[...], b_ref[...])
        h_ref[...] = x
    hb = x.astype(BF16)
    kv_off = (OFF_KA, OFF_VA, OFF_KB, OFF_VB)
    step = 256
    for c in range(0, IN_WIDTH, step):
        acc = _dot(hb, w_ref[:, c:c + step])
        zb_ref[:, c:c + step] = acc.astype(BF16)
        for ref, off in zip(kv_refs, kv_off):
            lo, hi = max(c, off), min(c + step, off + W_ATT)
            if lo < hi:
                ref[:, lo - off:hi - off] = acc[:, lo - c:hi - c]
                if with_kmean and off == OFF_KA:
                    for r in range(tm // MOBA_BLOCK):
                        blk = acc[r * MOBA_BLOCK:(r + 1) * MOBA_BLOCK, lo - c:hi - c]
                        km_ref[r:r + 1, lo - off:hi - off] = (
                            jnp.sum(blk, axis=0, keepdims=True) * (1.0 / MOBA_BLOCK))


def _in_proj(x, w, ln, kv_prev, layer, depth, with_kmean):
    n = x.shape[0]
    tm = min(512, n)
    row = lambda i: (i, 0)
    const = lambda i: (0, 0)
    args, in_specs = [x], [pl.BlockSpec((tm, D_MODEL), row)]
    if ln is not None:
        args += [ln[0].reshape(1, D_MODEL), ln[1].reshape(1, D_MODEL)]
        in_specs += [pl.BlockSpec((1, D_MODEL), const)] * 2
    args.append(w)
    in_specs.append(pl.BlockSpec((D_MODEL, IN_WIDTH), const))
    out_shape, out_specs = [], []
    if ln is not None:
        out_shape.append(jax.ShapeDtypeStruct((n, D_MODEL), F32))
        out_specs.append(pl.BlockSpec((tm, D_MODEL), row))
    out_shape.append(jax.ShapeDtypeStruct((n, IN_WIDTH), BF16))
    out_specs.append(pl.BlockSpec((tm, IN_WIDTH), row))
    aliases = {}
    for t in range(4):
        if kv_prev is not None:
            aliases[len(args)] = len(out_shape)
            args.append(kv_prev[t])
            in_specs.append(pl.BlockSpec(memory_space=pl.ANY))
        out_shape.append(jax.ShapeDtypeStruct((depth, n, W_ATT), F32))
        out_specs.append(pl.BlockSpec((None, tm, W_ATT), lambda i: (layer, i, 0)))
    if with_kmean:
        out_shape.append(jax.ShapeDtypeStruct((n // tm, tm // MOBA_BLOCK, W_ATT), F32))
        out_specs.append(pl.BlockSpec((None, tm // MOBA_BLOCK, W_ATT), lambda i: (i, 0, 0)))
    body = functools.partial(_in_proj_body, apply_ln=ln is not None,
                             n_alias=0 if kv_prev is None else 4,
                             with_kmean=with_kmean, tm=tm)
    outs = pl.pallas_call(
        body, grid=(n // tm,), in_specs=in_specs, out_specs=out_specs, out_shape=out_shape,
        input_output_aliases=aliases, compiler_params=_params(("parallel",)),
        name="in_proj")(*args)
    outs = list(outs)
    h = outs.pop(0) if ln is not None else x
    zb = outs.pop(0)
    kv = outs[:4]
    kmean = outs[4] if with_kmean else None
    return h, zb, kv, kmean


def _gate_body(q_ref, km_ref, o_ref):
    cur = pl.program_id(1)
    g = _dot(q_ref[...].astype(F32), km_ref[...], precision=lax.Precision.HIGHEST)
    blk = lax.broadcasted_iota(jnp.int32, g.shape, 1) & 15
    past = blk < cur
    g = jnp.where(past, g, -jnp.inf)
    rank = jnp.zeros(g.shape, jnp.int32)
    for s in range(1, 16):
        lower = pltpu.roll(g, s, axis=1)
        rank += jnp.where((blk >= s) & (lower >= g), 1, 0)
        upper = pltpu.roll(g, LANES - s, axis=1)
        rank += jnp.where((blk + s <= 15) & (upper > g), 1, 0)
    o_ref[...] = jnp.where(past & (rank < MOBA_TOPK), 0.0, NEG_BIG)


def _moba_gate(zb, km, batch, seq):
    nq = seq // MOBA_BLOCK
    return pl.pallas_call(
        _gate_body, grid=(batch, nq),
        in_specs=[pl.BlockSpec((MOBA_BLOCK, W_ATT), lambda b, i: (b * nq + i, OFF_QA // W_ATT)),
                  pl.BlockSpec((None, W_ATT, LANES), lambda b, i: (b, 0, 0))],
        out_specs=pl.BlockSpec((MOBA_BLOCK, LANES), lambda b, i: (b * nq + i, 0)),
        out_shape=jax.ShapeDtypeStruct((batch * seq, LANES), F32),
        compiler_params=_params(("parallel", "parallel")), name="moba_gate")(zb, km)


def _moba_body(slopes_ref, q_ref, k_ref, v_ref, selb_ref, o_ref):
    p = pl.program_id(1)
    i = pl.program_id(2)
    tq = MOBA_BLOCK
    q = q_ref[...].astype(F32)
    selb = selb_ref[...].astype(BF16)
    lane = lax.broadcasted_iota(jnp.int32, (tq, LANES), 1)
    rq = lax.broadcasted_iota(jnp.int32, (tq, tq), 0)
    ck = lax.broadcasted_iota(jnp.int32, (tq, tq), 1)
    d0 = (rq - ck).astype(F32)
    causal = ck <= rq
    row0 = pl.multiple_of(i * tq, tq)
    outs = []
    for hh in range(2):
        head = 2 * p + hh
        slope = slopes_ref[head]
        in_head = (lane >= HEAD_DIM * hh) & (lane < HEAD_DIM * (hh + 1))
        qh = jnp.where(in_head, q * (HEAD_DIM ** -0.5), 0.0).astype(BF16)
        lhs = jnp.concatenate([qh, selb], axis=1)
        sd = slope * d0

        kb = k_ref[pl.ds(row0, tq), :]
        vb = v_ref[pl.ds(row0, tq), :]
        s = jnp.where(causal, _dot_nt(qh, kb) - sd, -jnp.inf)
        m = jnp.max(s, axis=1, keepdims=True)
        pr = jnp.exp(s - m)
        l = jnp.sum(pr, axis=1, keepdims=True)
        acc = _dot(pr.astype(BF16), vb)

        def body(j, carry, lhs=lhs, sd=sd, slope=slope, head=head):
            m, l, acc = carry
            r0 = pl.multiple_of(j * tq, tq)
            kb = k_ref[pl.ds(r0, tq), :]
            vb = v_ref[pl.ds(r0, tq), :]
            onehot = jnp.where(lane == head * 16 + j, 1.0, 0.0).astype(BF16)
            rhs = jnp.concatenate([kb, onehot], axis=1)
            t = _dot_nt(lhs, rhs) - sd
            c = slope * ((i - j) * tq).astype(F32)
            m_new = jnp.maximum(m, jnp.max(t, axis=1, keepdims=True) - c)
            alpha = jnp.exp(m - m_new)
            pr = jnp.exp(t - (m_new + c))
            l = alpha * l + jnp.sum(pr, axis=1, keepdims=True)
            acc = alpha * acc + _dot(pr.astype(BF16), vb)
            return m_new, l, acc

        m, l, acc = lax.fori_loop(0, i, body, (m, l, acc))
        outs.append(acc / l)
    o_ref[...] = jnp.where(lane < HEAD_DIM, outs[0], outs[1]).astype(o_ref.dtype)


def _moba_prompt(zb, selb, slopes, batch, seq):
    nq = seq // MOBA_BLOCK
    cq, ck, cv = OFF_QA // LANES, OFF_KA // LANES, OFF_VA // LANES
    grid_spec = pltpu.PrefetchScalarGridSpec(
        num_scalar_prefetch=1, grid=(batch, N_HEADS // 2, nq),
        in_specs=[pl.BlockSpec((MOBA_BLOCK, LANES), lambda b, p, i, s: (b * nq + i, cq + p)),
                  pl.BlockSpec((seq, LANES), lambda b, p, i, s: (b, ck + p)),
                  pl.BlockSpec((seq, LANES), lambda b, p, i, s: (b, cv + p)),
                  pl.BlockSpec((MOBA_BLOCK, LANES), lambda b, p, i, s: (b * nq + i, 0))],
        out_specs=pl.BlockSpec((MOBA_BLOCK, LANES), lambda b, p, i, s: (b * nq + i, p)))
    return pl.pallas_call(
        _moba_body, grid_spec=grid_spec,
        out_shape=jax.ShapeDtypeStruct((batch * seq, W_ATT), BF16),
        compiler_params=_params(("parallel", "parallel", "arbitrary")),
        name="moba_prompt")(slopes, zb, zb, zb, selb)


def _neg_softplus(z):
    return -(jnp.maximum(z, 0.0) + jnp.log1p(jnp.exp(-jnp.abs(z))))


def _suffix_sums(lk, upper):
    hi = lk.astype(BF16)
    lo = (lk - hi.astype(F32)).astype(BF16)
    return _dot(hi, upper) + _dot(lo, upper)


def _sb_body(q_ref, k_ref, v_ref, o_ref):
    i = pl.program_id(2)
    tq = MOBA_BLOCK
    q = q_ref[...].astype(F32)
    lane = lax.broadcasted_iota(jnp.int32, (tq, LANES), 1)
    rq = lax.broadcasted_iota(jnp.int32, (tq, tq), 0)
    ck = lax.broadcasted_iota(jnp.int32, (tq, tq), 1)
    before = ck < rq
    upper = jnp.where(rq > ck, 1.0, 0.0).astype(BF16)
    outs = []
    for hh in range(2):
        in_head = (lane >= HEAD_DIM * hh) & (lane < HEAD_DIM * (hh + 1))
        qh = jnp.where(in_head, q * (HEAD_DIM ** -0.5), 0.0).astype(BF16)

        def block(j, carry, acc, diag, qh=qh):
            r0 = pl.multiple_of(j * tq, tq)
            kb = k_ref[pl.ds(r0, tq), :]
            vb = v_ref[pl.ds(r0, tq), :]
            z = _dot_nt(qh, kb)
            lk = _neg_softplus(z)
            if diag:
                lk = jnp.where(before, lk, 0.0)
            aft = _suffix_sums(lk, upper)
            a = jnp.exp(z + lk + aft + carry)
            if diag:
                a = jnp.where(before, a, 0.0)
            acc = acc + _dot(a.astype(BF16), vb)
            carry = carry + aft[:, 0:1] + lk[:, 0:1]
            return carry, acc

        carry, acc = block(i, jnp.zeros((tq, 1), F32), jnp.zeros((tq, LANES), F32), True)

        def cond(st):
            return (st[0] >= 0) & (st[1] > 0)

        def body(st, block=block):
            j, _, carry, acc = st
            carry, acc = block(j, carry, acc, False)
            live = (jnp.max(carry) > SB_EXIT).astype(jnp.int32)
            return j - 1, live, carry, acc

        live0 = (jnp.max(carry) > SB_EXIT).astype(jnp.int32)
        _, _, _, acc = lax.while_loop(cond, body, (i - 1, live0, carry, acc))
        outs.append(acc)
    o_ref[...] = jnp.where(lane < HEAD_DIM, outs[0], outs[1]).astype(o_ref.dtype)


def _sb_prompt(zb, batch, seq):
    nq = seq // MOBA_BLOCK
    cq, ck, cv = OFF_QB // LANES, OFF_KB // LANES, OFF_VB // LANES
    return pl.pallas_call(
        _sb_body, grid=(batch, N_HEADS // 2, nq),
        in_specs=[pl.BlockSpec((MOBA_BLOCK, LANES), lambda b, p, i: (b * nq + i, cq + p)),
                  pl.BlockSpec((seq, LANES), lambda b, p, i: (b, ck + p)),
                  pl.BlockSpec((seq, LANES), lambda b, p, i: (b, cv + p))],
        out_specs=pl.BlockSpec((MOBA_BLOCK, LANES), lambda b, p, i: (b * nq + i, p)),
        out_shape=jax.ShapeDtypeStruct((batch * seq, W_ATT), BF16),
        compiler_params=_params(("parallel", "parallel", "arbitrary")),
        name="sb_prompt")(zb, zb, zb)


def _gmlp_body(u_ref, v_ref, ws_ref, bias_ref, g_ref, b_ref, o_ref, *vn_refs, rows):
    hp = lax.Precision.HIGHEST
    gi = lax.broadcasted_iota(jnp.int32, (W_C, W_C), 0) // HEAD_DIM
    gj = lax.broadcasted_iota(jnp.int32, (W_C, W_C), 1) // HEAD_DIM
    gmean = jnp.where(gi == gj, 1.0 / HEAD_DIM, 0.0).astype(F32)
    ti = lax.broadcasted_iota(jnp.int32, (CHUNK, CHUNK), 0)
    tj = lax.broadcasted_iota(jnp.int32, (CHUNK, CHUNK), 1)
    lane = lax.broadcasted_iota(jnp.int32, (CHUNK, LANES), 1)
    wpair = []
    for pr in range(2):
        w0 = jnp.where(tj <= ti, ws_ref[2 * pr], 0.0)
        w1 = jnp.where(tj <= ti, ws_ref[2 * pr + 1], 0.0)
        wpair.append(jnp.concatenate([w0, w1], axis=1).astype(BF16))
    for r in range(rows // CHUNK):
        sl = slice(r * CHUNK, (r + 1) * CHUNK)
        u = jax.nn.gelu(u_ref[sl, :].astype(F32))
        v = jax.nn.gelu(v_ref[sl, :].astype(F32))
        mu = _dot(v, gmean, precision=hp)
        d = v - mu
        var = _dot(d * d, gmean, precision=hp)
        vn = d * lax.rsqrt(var + LN_EPS) * g_ref[...] + b_ref[...]
        if vn_refs:
            vn_refs[0][sl, :] = vn
        mixes = []
        for pr in range(2):
            vp = vn[:, pr * LANES:(pr + 1) * LANES]
            rhs = jnp.concatenate([jnp.where(lane < HEAD_DIM, vp, 0.0),
                                   jnp.where(lane >= HEAD_DIM, vp, 0.0)], axis=0).astype(BF16)
            mixes.append(_dot(wpair[pr], rhs))
        mix = jnp.concatenate(mixes, axis=1) + bias_ref[...]
        o_ref[sl, :] = (u * mix).astype(o_ref.dtype)


def _gmlp(zb, ws, bias, g, b, want_vn):
    n = zb.shape[0]
    rows = min(512, n)
    row_u = lambda i: (i, OFF_UC // W_C)
    row_v = lambda i: (i, OFF_VC // W_C)
    out_shape = [jax.ShapeDtypeStruct((n, W_C), BF16)]
    out_specs = [pl.BlockSpec((rows, W_C), lambda i: (i, 0))]
    if want_vn:
        out_shape.append(jax.ShapeDtypeStruct((n, W_C), F32))
        out_specs.append(pl.BlockSpec((rows, W_C), lambda i: (i, 0)))
    outs = pl.pallas_call(
        functools.partial(_gmlp_body, rows=rows), grid=(n // rows,),
        in_specs=[pl.BlockSpec((rows, W_C), row_u), pl.BlockSpec((rows, W_C), row_v),
                  pl.BlockSpec((C_GROUPS, CHUNK, CHUNK), lambda i: (0, 0, 0)),
                  pl.BlockSpec((CHUNK, W_C), lambda i: (0, 0)),
                  pl.BlockSpec((1, W_C), lambda i: (0, 0)),
                  pl.BlockSpec((1, W_C), lambda i: (0, 0))],
        out_specs=out_specs, out_shape=out_shape,
        compiler_params=_params(("parallel",)), name="gmlp")(
            zb, zb, ws, bias, g.reshape(1, W_C), b.reshape(1, W_C))
    return outs if want_vn else (outs[0], None)


def _out_proj_body(ma_ref, mb_ref, mc_ref, h_ref, w_ref, g_ref, b_ref, o_ref, *, alpha):
    mixed = jnp.concatenate([ma_ref[...].astype(BF16), mb_ref[...].astype(BF16),
                             mc_ref[...].astype(BF16)], axis=1)
    y = _dot(mixed, w_ref[...])
    o_ref[...] = _ln(alpha * h_ref[...] + y, g_ref[...], b_ref[...])


def _out_proj(ma, mb, mc, h, w, g, b, alpha):
    n = h.shape[0]
    tm = min(512, n)
    row = lambda i: (i, 0)
    const = lambda i: (0, 0)
    return pl.pallas_call(
        functools.partial(_out_proj_body, alpha=alpha), grid=(n // tm,),
        in_specs=[pl.BlockSpec((tm, W_ATT), row), pl.BlockSpec((tm, W_ATT), row),
                  pl.BlockSpec((tm, W_C), row), pl.BlockSpec((tm, D_MODEL), row),
                  pl.BlockSpec((D_MODEL, D_MODEL), const),
                  pl.BlockSpec((1, D_MODEL), const), pl.BlockSpec((1, D_MODEL), const)],
        out_specs=pl.BlockSpec((tm, D_MODEL), row),
        out_shape=jax.ShapeDtypeStruct((n, D_MODEL), F32),
        compiler_params=_params(("parallel",)), name="out_proj")(
            ma, mb, mc, h, w, g.reshape(1, D_MODEL), b.reshape(1, D_MODEL))


def _ffn_body(h_ref, w1_ref, w2_ref, g_ref, b_ref, o_ref, acc_ref, *, alpha):
    h = h_ref[...]
    hb = h.astype(BF16)
    step = 512
    for c in range(0, D_FF, step):
        a = _dot(hb, w1_ref[:, c:c + step])
        a = jnp.square(jnp.maximum(a, 0.0)).astype(BF16)
        y = _dot(a, w2_ref[c:c + step, :])
        if c == 0:
            acc_ref[...] = y
        else:
            acc_ref[...] += y
    o_ref[...] = _ln(alpha * h + acc_ref[...], g_ref[...], b_ref[...])


def _ffn(h, w1, w2, g, b, alpha):
    n = h.shape[0]
    tm = min(512, n)
    row = lambda i: (i, 0)
    const = lambda i: (0, 0)
    return pl.pallas_call(
        functools.partial(_ffn_body, alpha=alpha), grid=(n // tm,),
        in_specs=[pl.BlockSpec((tm, D_MODEL), row),
                  pl.BlockSpec((D_MODEL, D_FF), const, pipeline_mode=pl.Buffered(1)),
                  pl.BlockSpec((D_FF, D_MODEL), const, pipeline_mode=pl.Buffered(1)),
                  pl.BlockSpec((1, D_MODEL), const), pl.BlockSpec((1, D_MODEL), const)],
        out_specs=pl.BlockSpec((tm, D_MODEL), row),
        out_shape=jax.ShapeDtypeStruct((n, D_MODEL), F32),
        scratch_shapes=[pltpu.VMEM((tm, D_MODEL), F32)],
        compiler_params=_params(("parallel",)), name="ffn")(
            h, w1, w2, g.reshape(1, D_MODEL), b.reshape(1, D_MODEL))


ROWS_S = 32


def _expand_heads(q):
    t_new = q.shape[0]
    r = lax.broadcasted_iota(jnp.int32, (8, W_ATT), 0)
    c = lax.broadcasted_iota(jnp.int32, (8, W_ATT), 1) // HEAD_DIM
    pieces = [jnp.where(r == c, jnp.broadcast_to(q[t:t + 1, :], (8, W_ATT)), 0.0)
              for t in range(t_new)]
    return jnp.concatenate(pieces, axis=0)


def _reduce_heads(o, out_ref):
    r = lax.broadcasted_iota(jnp.int32, (8, W_ATT), 0)
    c = lax.broadcasted_iota(jnp.int32, (8, W_ATT), 1) // HEAD_DIM
    for t in range(o.shape[0] // 8):
        piece = jnp.where(r == c, o[8 * t:8 * t + 8, :], 0.0)
        out_ref[t:t + 1, :] = jnp.sum(piece, axis=0, keepdims=True).astype(out_ref.dtype)


def _sample_attn_body(pt_ref, qa_ref, kan_ref, van_ref, qb_ref, kbn_ref, vbn_ref,
                      ka_ref, va_ref, kb_ref, vb_ref, oa_ref, ob_ref,
                      m_s, l_s, acc_s, ks_s, carry_s, accb_s, live_s,
                      *, n_pages, t_new, slopes):
    g = pl.program_id(1)
    page = n_pages - 1 - g
    past_len = n_pages * PAGE
    n_blk = past_len // MOBA_BLOCK
    scale = HEAD_DIM ** -0.5
    rows = 8 * t_new
    ri = lax.broadcasted_iota(jnp.int32, (rows, 1), 0)
    tok = ri // 8
    hd = ri % 8
    slope = jnp.zeros((rows, 1), F32)
    for h in range(N_HEADS):
        slope = jnp.where(hd == h, float(slopes[h]), slope)
    qa = _expand_heads(qa_ref[...].astype(F32) * scale)
    qb = _expand_heads(qb_ref[...].astype(F32) * scale)
    upper = jnp.where(lax.broadcasted_iota(jnp.int32, (PAGE, PAGE), 0) >
                      lax.broadcasted_iota(jnp.int32, (PAGE, PAGE), 1), 1.0, 0.0).astype(BF16)

    @pl.when(g == 0)
    def _init():
        ks_s[...] = jnp.zeros_like(ks_s)
        kan = kan_ref[...].astype(F32)
        van = van_ref[...].astype(F32)
        kbn = kbn_ref[...].astype(F32)
        vbn = vbn_ref[...].astype(F32)
        sa, zs = [], []
        for s in range(t_new):
            sc = jnp.sum(qa * kan[s:s + 1, :], axis=1, keepdims=True)
            sc = sc - slope * (tok - s).astype(F32)
            sa.append(jnp.where(tok >= s, sc, -jnp.inf))
            zs.append(jnp.sum(qb * kbn[s:s + 1, :], axis=1, keepdims=True))
        m = sa[0]
        for s in range(1, t_new):
            m = jnp.maximum(m, sa[s])
        l = jnp.zeros((rows, 1), F32)
        acc = jnp.zeros((rows, W_ATT), F32)
        for s in range(t_new):
            pr = jnp.exp(sa[s] - m)
            l = l + pr
            acc = acc + pr * van[s:s + 1, :]
        m_s[n_pages] = m
        l_s[n_pages] = l
        acc_s[n_pages] = acc
        carry = jnp.zeros((rows, 1), F32)
        accb = jnp.zeros((rows, W_ATT), F32)
        for s in range(t_new - 1, -1, -1):
            valid = tok > s
            lk = jnp.where(valid, _neg_softplus(zs[s]), 0.0)
            a = jnp.where(valid, jnp.exp(zs[s] + lk + carry), 0.0)
            accb = accb + a * vbn[s:s + 1, :]
            carry = carry + lk
        carry_s[...] = carry
        accb_s[...] = accb
        live_s[0] = (jnp.max(carry) > SB_EXIT).astype(jnp.int32)

    kpos0 = page * PAGE
    kaf = ka_ref[...]
    sc = _dot_nt(qa.astype(BF16), kaf.astype(BF16))
    kcol = lax.broadcasted_iota(jnp.int32, (rows, PAGE), 1)
    dist = (past_len + tok - kpos0 - kcol).astype(F32)
    sc = sc - slope * dist
    m = jnp.max(sc, axis=1, keepdims=True)
    pr = jnp.exp(sc - m)
    m_s[page] = m
    l_s[page] = jnp.sum(pr, axis=1, keepdims=True)
    acc_s[page] = _dot(pr.astype(BF16), va_ref[...].astype(BF16))
    blk = page // (MOBA_BLOCK // PAGE)
    ks_s[blk] = ks_s[blk] + jnp.sum(kaf, axis=0, keepdims=True)

    @pl.when(live_s[0] > 0)
    def _sb():
        z = _dot_nt(qb.astype(BF16), kb_ref[...].astype(BF16))
        lk = _neg_softplus(z)
        aft = _suffix_sums(lk, upper)
        carry = carry_s[...]
        a = jnp.exp(z + lk + aft + carry)
        accb_s[...] = accb_s[...] + _dot(a.astype(BF16), vb_ref[...].astype(BF16))
        carry = carry + aft[:, 0:1] + lk[:, 0:1]
        carry_s[...] = carry
        live_s[0] = (jnp.max(carry) > SB_EXIT).astype(jnp.int32)

    @pl.when(g == n_pages - 1)
    def _finish():
        gates = [jnp.sum(qa * ks_s[nb], axis=1, keepdims=True) for nb in range(n_blk)]
        sel = []
        for nb in range(n_blk):
            rank = jnp.zeros((rows, 1), jnp.int32)
            for mth in range(n_blk):
                if mth < nb:
                    rank += jnp.where(gates[mth] >= gates[nb], 1, 0)
                elif mth > nb:
                    rank += jnp.where(gates[mth] > gates[nb], 1, 0)
            sel.append(rank < MOBA_TOPK)
        pages_per_blk = MOBA_BLOCK // PAGE
        big = m_s[n_pages]
        for pg in range(n_pages):
            big = jnp.maximum(big, jnp.where(sel[pg // pages_per_blk], m_s[pg], -jnp.inf))
        w = jnp.exp(m_s[n_pages] - big)
        l = w * l_s[n_pages]
        acc = w * acc_s[n_pages]
        for pg in range(n_pages):
            w = jnp.where(sel[pg // pages_per_blk],
                          jnp.exp(jnp.minimum(m_s[pg] - big, 0.0)), 0.0)
            l = l + w * l_s[pg]
            acc = acc + w * acc_s[pg]
        _reduce_heads(acc / l, oa_ref)
        _reduce_heads(accb_s[...], ob_ref)


def _sample_attn(zb3, caches, layer, page_table, slopes):
    bsz, t_new, _ = zb3.shape
    n_pages = page_table.shape[1]
    rows = 8 * t_new
    tok = lambda c: pl.BlockSpec((None, t_new, W_ATT), lambda b, g, pt: (b, 0, c))
    cache = pl.BlockSpec(
        (None, None, PAGE, W_ATT),
        lambda b, g, pt: (layer, pt[b * n_pages + n_pages - 1 - g], 0, 0))
    grid_spec = pltpu.PrefetchScalarGridSpec(
        num_scalar_prefetch=1, grid=(bsz, n_pages),
        in_specs=[tok(OFF_QA // W_ATT), tok(OFF_KA // W_ATT), tok(OFF_VA // W_ATT),
                  tok(OFF_QB // W_ATT), tok(OFF_KB // W_ATT), tok(OFF_VB // W_ATT),
                  cache, cache, cache, cache],
        out_specs=[pl.BlockSpec((None, t_new, W_ATT), lambda b, g, pt: (b, 0, 0))] * 2,
        scratch_shapes=[pltpu.VMEM((n_pages + 1, rows, 1), F32),
                        pltpu.VMEM((n_pages + 1, rows, 1), F32),
                        pltpu.VMEM((n_pages + 1, rows, W_ATT), F32),
                        pltpu.VMEM((n_pages * PAGE // MOBA_BLOCK, 1, W_ATT), F32),
                        pltpu.VMEM((rows, 1), F32),
                        pltpu.VMEM((rows, W_ATT), F32),
                        pltpu.SMEM((1,), jnp.int32)])
    body = functools.partial(_sample_attn_body, n_pages=n_pages, t_new=t_new,
                             slopes=tuple(float(s) for s in slopes))
    return pl.pallas_call(
        body, grid_spec=grid_spec,
        out_shape=[jax.ShapeDtypeStruct((bsz, t_new, W_ATT), F32)] * 2,
        compiler_params=_params(("parallel", "arbitrary")), name="sample_attn")(
            page_table.reshape(-1), zb3, zb3, zb3, zb3, zb3, zb3, *caches)


def _gate_matrix(kmean, batch, n_blk):
    km = kmean.reshape(batch, n_blk, N_HEADS, HEAD_DIM).transpose(0, 2, 3, 1)
    km = jnp.pad(km, ((0, 0), (0, 0), (0, 0), (0, 16 - n_blk)))
    eye = jnp.eye(N_HEADS, LANES // 16, dtype=F32)
    out = km[:, :, :, None, :] * eye[None, :, None, :, None]
    return out.reshape(batch, W_ATT, LANES)


def kernel(x_prompt, x_sample, cache_k_a, cache_v_a, cache_k_b, cache_v_b, page_table, ln_in_g, ln_in_b, w_in, w_out, w_spatial, b_spatial, ln_c_g, ln_c_b, ln1_g, ln1_b, w_ff1, w_ff2, ln2_g, ln2_b):
    batch, seq, _ = x_prompt.shape
    dec_batch, dec_seq, _ = x_sample.shape
    depth = w_in.shape[0]
    alpha = (2 * depth) ** 0.25
    slopes_np = _alibi_slopes(N_HEADS)
    slopes = jnp.asarray(slopes_np)
    n_blk = seq // MOBA_BLOCK
    assert n_blk <= 16 and seq % 512 == 0 and dec_seq <= CHUNK

    w_in_b = w_in.astype(BF16)
    w_out_b = w_out.astype(BF16)
    w_ff1_b = w_ff1.astype(BF16)
    w_ff2_b = w_ff2.astype(BF16)
    caches = [c.reshape(c.shape[0], c.shape[1], PAGE, W_ATT)
              for c in (cache_k_a, cache_v_a, cache_k_b, cache_v_b)]

    reps = CHUNK // dec_seq
    eye = jnp.eye(reps, dtype=F32)
    ws_s = jnp.einsum('ab,lgts->lgatbs', eye, w_spatial[:, :, :dec_seq, :dec_seq]).reshape(
        depth, C_GROUPS, CHUNK, CHUNK)
    bias_p = jnp.repeat(jnp.swapaxes(b_spatial, 1, 2), HEAD_DIM, axis=2)
    bias_s = jnp.tile(jnp.repeat(jnp.swapaxes(b_spatial[:, :, :dec_seq], 1, 2), HEAD_DIM, axis=2),
                      (1, reps, 1))

    hp = x_prompt.reshape(batch * seq, D_MODEL)
    hs = x_sample.reshape(dec_batch * dec_seq, D_MODEL)
    kv_p = kv_s = None
    vc_s = []
    for l in range(depth):
        ln = (ln_in_g, ln_in_b) if l == 0 else None
        lnc_g, lnc_b = ln_c_g[l].reshape(-1), ln_c_b[l].reshape(-1)

        hp, zb, kv_p, kmean = _in_proj(hp, w_in_b[l], ln, kv_p, l, depth, True)
        selb = _moba_gate(zb, _gate_matrix(kmean, batch, n_blk), batch, seq)
        mix_a = _moba_prompt(zb, selb, slopes, batch, seq)
        mix_b = _sb_prompt(zb, batch, seq)
        mix_c, _ = _gmlp(zb, w_spatial[l], bias_p[l], lnc_g, lnc_b, False)
        h1 = _out_proj(mix_a, mix_b, mix_c, hp, w_out_b[l], ln1_g[l], ln1_b[l], alpha)
        hp = _ffn(h1, w_ff1_b[l], w_ff2_b[l], ln2_g[l], ln2_b[l], alpha)

        hs, zs, kv_s, _ = _in_proj(hs, w_in_b[l], ln, kv_s, l, depth, False)
        oa, ob = _sample_attn(zs.reshape(dec_batch, dec_seq, IN_WIDTH), caches, l,
                              page_table, slopes_np)
        mix_c, vn = _gmlp(zs, ws_s[l], bias_s[l], lnc_g, lnc_b, True)
        vc_s.append(vn)
        h1 = _out_proj(oa.reshape(-1, W_ATT), ob.reshape(-1, W_ATT), mix_c, hs,
                       w_out_b[l], ln1_g[l], ln1_b[l], alpha)
        hs = _ffn(h1, w_ff1_b[l], w_ff2_b[l], ln2_g[l], ln2_b[l], alpha)

    shape_p = (depth, batch, seq, N_HEADS, HEAD_DIM)
    shape_s = (depth, dec_batch, dec_seq, N_HEADS, HEAD_DIM)
    return (hp.reshape(batch, seq, D_MODEL), hs.reshape(dec_batch, dec_seq, D_MODEL),
            kv_p[0].reshape(shape_p), kv_p[1].reshape(shape_p),
            kv_p[2].reshape(shape_p), kv_p[3].reshape(shape_p),
            kv_s[0].reshape(shape_s), kv_s[1].reshape(shape_s),
            kv_s[2].reshape(shape_s), kv_s[3].reshape(shape_s),
            jnp.stack(vc_s).reshape(depth, dec_batch, dec_seq, W_C))
```

```python
import functools
import math

import jax
import jax.numpy as jnp
import numpy as np
from jax import lax
from jax.experimental import pallas as pl
from jax.experimental.pallas import tpu as pltpu

F32 = jnp.float32
BF16 = jnp.bfloat16

D_MODEL = 1024
HEAD_DIM = 64
N_HEADS = 6
W_ATT = N_HEADS * HEAD_DIM
C_GROUPS = 4
W_C = C_GROUPS * HEAD_DIM
IN_WIDTH = 6 * W_ATT + 2 * W_C
D_FF = 4 * D_MODEL
MOBA_BLOCK = 256
MOBA_TOPK = 3
CHUNK = 128
PAGE = 128
LN_EPS = 1e-5
LANES = 128
NEG_BIG = -1e30
SB_EXIT = -110.0

OFF_QA, OFF_KA, OFF_VA = 0, W_ATT, 2 * W_ATT
OFF_QB, OFF_KB, OFF_VB = 3 * W_ATT, 4 * W_ATT, 5 * W_ATT
OFF_UC, OFF_VC = 6 * W_ATT, 6 * W_ATT + W_C
ZP_QA, ZP_QB, ZP_UC, ZP_VC = 0, W_ATT, 2 * W_ATT, 2 * W_ATT + W_C
ZP_WIDTH = 2 * W_ATT + 2 * W_C

VMEM_LIMIT = 56 * 1024 * 1024


def _alibi_slopes(n):
    def pow2(m):
        start = 2.0 ** (-8.0 / m)
        return [start ** (i + 1) for i in range(m)]
    p = 2 ** int(math.floor(math.log2(n)))
    s = pow2(p)
    if p < n:
        s = s + pow2(2 * p)[0::2][: n - p]
    return np.array(s, dtype=np.float32)


def _ln(x, g, b):
    mu = jnp.mean(x, axis=-1, keepdims=True)
    d = x - mu
    var = jnp.mean(d * d, axis=-1, keepdims=True)
    return d * lax.rsqrt(var + LN_EPS) * g + b


def _dot(a, b, precision=None):
    return jnp.dot(a, b, preferred_element_type=F32, precision=precision)


def _dot_nt(a, b, precision=None):
    return lax.dot_general(a, b, (((1,), (1,)), ((), ())),
                           preferred_element_type=F32, precision=precision)


def _params(sem):
    return pltpu.CompilerParams(dimension_semantics=sem, vmem_limit_bytes=VMEM_LIMIT)


def _in_proj_prompt_body(*refs, apply_ln, n_alias, tm):
    it = iter(refs)
    x_ref = next(it)
    if apply_ln:
        g_ref, b_ref = next(it), next(it)
    wq_ref, wkv_ref = next(it), next(it)
    for _ in range(n_alias):
        next(it)
    if apply_ln:
        h_ref = next(it)
    zb_ref = next(it)
    kv_refs = [next(it) for _ in range(4)]
    km_ref = next(it)
    ti = pl.program_id(1)

    x = x_ref[...]
    if apply_ln:
        x = _ln(x, g_ref[...], b_ref[...])
        h_ref[...] = x
    hb = x.astype(BF16)
    for c in range(0, ZP_WIDTH, 256):
        zb_ref[:, c:c + 256] = _dot(hb, wq_ref[:, c:c + 256]).astype(BF16)
    lane = lax.broadcasted_iota(jnp.int32, (W_ATT, LANES), 1)

    @pl.when(ti == 0)
    def _zero_means():
        km_ref[...] = jnp.zeros_like(km_ref)

    for t in range(4):
        kt = _dot_nt(wkv_ref[t * W_ATT:(t + 1) * W_ATT, :], hb)
        kv_refs[t][...] = kt
        if t == 0:
            km = km_ref[...]
            for r in range(tm // MOBA_BLOCK):
                mean = jnp.sum(kt[:, r * MOBA_BLOCK:(r + 1) * MOBA_BLOCK], axis=1,
                               keepdims=True) * (1.0 / MOBA_BLOCK)
                km = jnp.where(lane == ti * (tm // MOBA_BLOCK) + r, mean, km)
            km_ref[...] = km


def _in_proj_prompt(x, wq, wkv, ln, kv_prev, layer, depth, batch, seq):
    n = x.shape[0]
    tm = 512
    nt = seq // tm
    row = lambda b, t: (b * nt + t, 0)
    const = lambda b, t: (0, 0)
    args, in_specs = [x], [pl.BlockSpec((tm, D_MODEL), row)]
    if ln is not None:
        args += [ln[0].reshape(1, D_MODEL), ln[1].reshape(1, D_MODEL)]
        in_specs += [pl.BlockSpec((1, D_MODEL), const)] * 2
    args += [wq, wkv]
    in_specs += [pl.BlockSpec((D_MODEL, ZP_WIDTH), const), pl.BlockSpec((4 * W_ATT, D_MODEL), const)]
    out_shape, out_specs = [], []
    if ln is not None:
        out_shape.append(jax.ShapeDtypeStruct((n, D_MODEL), F32))
        out_specs.append(pl.BlockSpec((tm, D_MODEL), row))
    out_shape.append(jax.ShapeDtypeStruct((n, ZP_WIDTH), BF16))
    out_specs.append(pl.BlockSpec((tm, ZP_WIDTH), row))
    aliases = {}
    for t in range(4):
        if kv_prev is not None:
            aliases[len(args)] = len(out_shape)
            args.append(kv_prev[t])
            in_specs.append(pl.BlockSpec(memory_space=pl.ANY))
        out_shape.append(jax.ShapeDtypeStruct((depth, batch, W_ATT, seq), F32))
        out_specs.append(pl.BlockSpec((None, None, W_ATT, tm), lambda b, t: (layer, b, 0, t)))
    out_shape.append(jax.ShapeDtypeStruct((batch, W_ATT, LANES), F32))
    out_specs.append(pl.BlockSpec((None, W_ATT, LANES), lambda b, t: (b, 0, 0)))
    body = functools.partial(_in_proj_prompt_body, apply_ln=ln is not None,
                             n_alias=0 if kv_prev is None else 4, tm=tm)
    outs = list(pl.pallas_call(
        body, grid=(batch, nt), in_specs=in_specs, out_specs=out_specs, out_shape=out_shape,
        input_output_aliases=aliases, compiler_params=_params(("parallel", "arbitrary")),
        name="in_proj_prompt")(*args))
    h = outs.pop(0) if ln is not None else x
    return h, outs[0], outs[1:5], outs[5]


def _in_proj_sample_body(*refs, apply_ln, n_alias, t_new, bsz):
    it = iter(refs)
    x_ref = next(it)
    if apply_ln:
        g_ref, b_ref = next(it), next(it)
    w_ref, wkv_ref = next(it), next(it)
    for _ in range(n_alias):
        next(it)
    if apply_ln:
        h_ref = next(it)
    zs_ref = next(it)
    kv_refs = [next(it) for _ in range(4)]

    x = x_ref[...]
    if apply_ln:
        x = _ln(x, g_ref[...], b_ref[...])
        h_ref[...] = x
    hb = x.astype(BF16)
    for c in range(0, IN_WIDTH, 256):
        zs_ref[:, c:c + 256] = _dot(hb, w_ref[:, c:c + 256])
    for t in range(4):
        kt = _dot_nt(wkv_ref[t * W_ATT:(t + 1) * W_ATT, :], hb)
        for tt in range(t_new):
            kv_refs[t][tt] = kt[:, tt * bsz:(tt + 1) * bsz]


def _in_proj_sample(x, w, wkv, ln, kv_prev, layer, depth, t_new, bsz):
    n = x.shape[0]
    const = lambda i: (0, 0)
    args, in_specs = [x], [pl.BlockSpec((n, D_MODEL), const)]
    if ln is not None:
        args += [ln[0].reshape(1, D_MODEL), ln[1].reshape(1, D_MODEL)]
        in_specs += [pl.BlockSpec((1, D_MODEL), const)] * 2
    args += [w, wkv]
    in_specs += [pl.BlockSpec((D_MODEL, IN_WIDTH), const), pl.BlockSpec((4 * W_ATT, D_MODEL), const)]
    out_shape, out_specs = [], []
    if ln is not None:
        out_shape.append(jax.ShapeDtypeStruct((n, D_MODEL), F32))
        out_specs.append(pl.BlockSpec((n, D_MODEL), const))
    out_shape.append(jax.ShapeDtypeStruct((n, IN_WIDTH), F32))
    out_specs.append(pl.BlockSpec((n, IN_WIDTH), const))
    aliases = {}
    for t in range(4):
        if kv_prev is not None:
            aliases[len(args)] = len(out_shape)
            args.append(kv_prev[t])
            in_specs.append(pl.BlockSpec(memory_space=pl.ANY))
        out_shape.append(jax.ShapeDtypeStruct((depth, t_new, W_ATT, bsz), F32))
        out_specs.append(pl.BlockSpec((None, t_new, W_ATT, bsz), lambda i: (layer, 0, 0, 0)))
    body = functools.partial(_in_proj_sample_body, apply_ln=ln is not None,
                             n_alias=0 if kv_prev is None else 4, t_new=t_new, bsz=bsz)
    outs = list(pl.pallas_call(
        body, grid=(1,), in_specs=in_specs, out_specs=out_specs, out_shape=out_shape,
        input_output_aliases=aliases, compiler_params=_params(("arbitrary",)),
        name="in_proj_sample")(*args))
    h = outs.pop(0) if ln is not None else x
    return h, outs[0], outs[1:5]


def _gate_body(q_ref, km_ref, o_ref):
    cur = pl.program_id(1)
    g = _dot(q_ref[...].astype(F32), km_ref[...], precision=lax.Precision.HIGHEST)
    blk = lax.broadcasted_iota(jnp.int32, g.shape, 1) & 15
    past = blk < cur
    g = jnp.where(past, g, -jnp.inf)
    rank = jnp.zeros(g.shape, jnp.int32)
    for s in range(1, 16):
        lower = pltpu.roll(g, s, axis=1)
        rank += jnp.where((blk >= s) & (lower >= g), 1, 0)
        upper = pltpu.roll(g, LANES - s, axis=1)
        rank += jnp.where((blk + s <= 15) & (upper > g), 1, 0)
    o_ref[...] = jnp.where(past & (rank < MOBA_TOPK), 0.0, NEG_BIG)


def _moba_gate(zb, km, batch, seq):
    nq = seq // MOBA_BLOCK
    return pl.pallas_call(
        _gate_body, grid=(batch, nq),
        in_specs=[pl.BlockSpec((MOBA_BLOCK, W_ATT), lambda b, i: (b * nq + i, ZP_QA // W_ATT)),
                  pl.BlockSpec((None, W_ATT, LANES), lambda b, i: (b, 0, 0))],
        out_specs=pl.BlockSpec((MOBA_BLOCK, LANES), lambda b, i: (b * nq + i, 0)),
        out_shape=jax.ShapeDtypeStruct((batch * seq, LANES), F32),
        compiler_params=_params(("parallel", "parallel")), name="moba_gate")(zb, km)


ROLE_SEL0, ROLE_SEL1 = 0, 1
ROLE_ROW, ROLE_KEY = 96, 97
ROLE_BLK, ROLE_2ND = 98, 99


ATT_PAIRS = 3


def _moba_body(slopes_ref, q_ref, kt_ref, vt_ref, selb_ref, o_ref):
    pg = pl.program_id(1)
    i = pl.program_id(2)
    tq = MOBA_BLOCK
    selb = selb_ref[...]
    lane = lax.broadcasted_iota(jnp.int32, (tq, LANES), 1)
    rowf = lax.broadcasted_iota(jnp.int32, (tq, LANES), 0).astype(F32)
    role = lax.broadcasted_iota(jnp.int32, (LANES, 2 * tq), 0)
    key2 = lax.broadcasted_iota(jnp.int32, (LANES, 2 * tq), 1)
    second = key2 >= tq
    keyf = jnp.where(second, key2 - tq, key2).astype(F32)
    rq = lax.broadcasted_iota(jnp.int32, (tq, tq), 0)
    ck = lax.broadcasted_iota(jnp.int32, (tq, tq), 1)
    d0 = (rq - ck).astype(F32)
    causal = ck <= rq
    row0 = pl.multiple_of(i * tq, tq)
    fixed = (lane == ROLE_ROW) | (lane == ROLE_KEY) | (lane == ROLE_2ND)

    heads, state = [], []
    for pr_i in range(ATT_PAIRS):
        lanes = slice(pr_i * LANES, (pr_i + 1) * LANES)
        q = q_ref[:, lanes].astype(F32)
        kd = kt_ref[lanes, pl.ds(row0, tq)].astype(BF16)
        vd = vt_ref[lanes, pl.ds(row0, tq)].astype(BF16)
        for hh in range(2):
            head = 2 * (pg * ATT_PAIRS + pr_i) + hh
            slope = slopes_ref[head]
            in_head = (lane >= HEAD_DIM * hh) & (lane < HEAD_DIM * (hh + 1))
            qh = jnp.where(in_head, q * (HEAD_DIM ** -0.5), 0.0).astype(BF16)
            lhs_fix = jnp.where(lane == ROLE_ROW, -slope * rowf, 1.0)
            rhs_roles = jnp.where(
                role == ROLE_SEL0, jnp.where(second, 0.0, 1.0),
                jnp.where(role == ROLE_SEL1, jnp.where(second, 1.0, 0.0),
                          jnp.where((role == ROLE_ROW) | (role == ROLE_BLK), 1.0,
                                    jnp.where(role == ROLE_KEY, slope * keyf,
                                              jnp.where((role == ROLE_2ND) & second,
                                                        slope * tq, 0.0))))).astype(BF16)
            heads.append((head, slope, qh, lhs_fix, rhs_roles))
            s = jnp.where(causal, _dot(qh, kd) - slope * d0, -jnp.inf)
            m = jnp.max(s, axis=1, keepdims=True)
            pr = jnp.exp(s - m)
            state += [m, jnp.sum(pr, axis=1, keepdims=True), _dot_nt(pr.astype(BF16), vd)]

    def body(pp, carry):
        j0 = 2 * pp
        r0 = pl.multiple_of(j0 * tq, 2 * tq)
        out = []
        for pr_i in range(ATT_PAIRS):
            lanes = slice(pr_i * LANES, (pr_i + 1) * LANES)
            k2 = kt_ref[lanes, pl.ds(r0, 2 * tq)].astype(BF16)
            v2 = vt_ref[lanes, pl.ds(r0, 2 * tq)].astype(BF16)
            for hh in range(2):
                hi = 2 * pr_i + hh
                head, slope, qh, lhs_fix, rhs_roles = heads[hi]
                m, l, acc = carry[3 * hi:3 * hi + 3]
                sel = pltpu.roll(selb, (LANES - head * 16 - j0) % LANES, axis=1)
                off = -slope * ((i - j0) * tq).astype(F32)
                ext = jnp.where(fixed, lhs_fix, jnp.where(lane == ROLE_BLK, off, sel))
                lhs = jnp.concatenate([qh, ext.astype(BF16)], axis=1)
                rhs = jnp.concatenate([k2, rhs_roles], axis=0)
                t = _dot(lhs, rhs)
                m_new = jnp.maximum(m, jnp.max(t, axis=1, keepdims=True))
                alpha = jnp.exp(m - m_new)
                pr = jnp.exp(t - m_new)
                l = alpha * l + jnp.sum(pr, axis=1, keepdims=True)
                acc = alpha * acc + _dot_nt(pr.astype(BF16), v2)
                out += [m_new, l, acc]
        return tuple(out)

    state = lax.fori_loop(0, (i + 1) // 2, body, tuple(state))
    for pr_i in range(ATT_PAIRS):
        o0 = state[6 * pr_i + 2] / state[6 * pr_i + 1]
        o1 = state[6 * pr_i + 5] / state[6 * pr_i + 4]
        o_ref[:, pr_i * LANES:(pr_i + 1) * LANES] = jnp.where(lane < HEAD_DIM, o0, o1).astype(o_ref.dtype)


def _moba_prompt(zb, kt, vt, selb, slopes, layer, batch, seq):
    nq = seq // MOBA_BLOCK
    wq = ATT_PAIRS * LANES
    cq = ZP_QA // wq
    kv_spec = pl.BlockSpec((None, None, wq, seq), lambda b, p, i, s: (layer, b, p, 0))
    grid_spec = pltpu.PrefetchScalarGridSpec(
        num_scalar_prefetch=1, grid=(batch, N_HEADS // 2 // ATT_PAIRS, nq),
        in_specs=[pl.BlockSpec((MOBA_BLOCK, wq), lambda b, p, i, s: (b * nq + i, cq + p)),
                  kv_spec, kv_spec,
                  pl.BlockSpec((MOBA_BLOCK, LANES), lambda b, p, i, s: (b * nq + i, 0))],
        out_specs=pl.BlockSpec((MOBA_BLOCK, wq), lambda b, p, i, s: (b * nq + i, p)))
    return pl.pallas_call(
        _moba_body, grid_spec=grid_spec,
        out_shape=jax.ShapeDtypeStruct((batch * seq, W_ATT), BF16),
        compiler_params=_params(("parallel", "parallel", "arbitrary")),
        name="moba_prompt")(slopes, zb, kt, vt, selb)


LOG2E = 1.4426950408889634
SB_EXIT2 = SB_EXIT * LOG2E


def _softplus2(y):
    return jnp.maximum(y, 0.0) + jnp.log2(1.0 + jnp.exp2(-jnp.abs(y)))


def _suffix_sums(lk, upper2):
    hi = lk.astype(BF16)
    lo = (lk - hi.astype(F32)).astype(BF16)
    return _dot(jnp.concatenate([hi, lo], axis=1), upper2)


def _upper2(n):
    r = lax.broadcasted_iota(jnp.int32, (2 * n, n), 0)
    c = lax.broadcasted_iota(jnp.int32, (2 * n, n), 1)
    return jnp.where(jnp.where(r >= n, r - n, r) > c, 1.0, 0.0).astype(BF16)


def _sb_body(q_ref, kt_ref, vt_ref, o_ref):
    i = pl.program_id(2)
    tq = MOBA_BLOCK
    lane = lax.broadcasted_iota(jnp.int32, (tq, LANES), 1)
    rq = lax.broadcasted_iota(jnp.int32, (tq, tq), 0)
    ck = lax.broadcasted_iota(jnp.int32, (tq, tq), 1)
    before = ck < rq
    upper2 = _upper2(tq)
    n_heads = 2 * ATT_PAIRS
    qhs = []
    for pr_i in range(ATT_PAIRS):
        q = q_ref[:, pr_i * LANES:(pr_i + 1) * LANES].astype(F32)
        for hh in range(2):
            in_head = (lane >= HEAD_DIM * hh) & (lane < HEAD_DIM * (hh + 1))
            qhs.append(jnp.where(in_head, q * (HEAD_DIM ** -0.5), 0.0).astype(BF16))

    def block(j, st, diag):
        r0 = pl.multiple_of(j * tq, tq)
        out = []
        for pr_i in range(ATT_PAIRS):
            lanes = slice(pr_i * LANES, (pr_i + 1) * LANES)
            kb = kt_ref[lanes, pl.ds(r0, tq)].astype(BF16)
            vb = vt_ref[lanes, pl.ds(r0, tq)].astype(BF16)
            for hh in range(2):
                hi = 2 * pr_i + hh
                carry, acc = st[2 * hi], st[2 * hi + 1]
                y = _dot(qhs[hi], kb) * LOG2E
                sp = _softplus2(y)
                lk = -sp
                if diag:
                    lk = jnp.where(before, lk, 0.0)
                aft = _suffix_sums(lk, upper2)
                a = jnp.exp2(y - sp + aft + carry)
                if diag:
                    a = jnp.where(before, a, 0.0)
                out += [carry + aft[:, 0:1] + lk[:, 0:1], acc + _dot_nt(a.astype(BF16), vb)]
        return tuple(out)

    def alive(st):
        top = st[0]
        for hi in range(1, n_heads):
            top = jnp.maximum(top, st[2 * hi])
        return (jnp.max(top) > SB_EXIT2).astype(jnp.int32)

    zc, za = jnp.zeros((tq, 1), F32), jnp.zeros((tq, LANES), F32)
    st = block(i, (zc, za) * n_heads, True)

    def cond(c):
        return (c[0] >= 0) & (c[1] > 0)

    def body(c):
        st = block(c[0], c[2:], False)
        return (c[0] - 1, alive(st)) + st

    res = lax.while_loop(cond, body, (i - 1, alive(st)) + st)
    for pr_i in range(ATT_PAIRS):
        o_ref[:, pr_i * LANES:(pr_i + 1) * LANES] = jnp.where(
            lane < HEAD_DIM, res[2 + 4 * pr_i + 1], res[2 + 4 * pr_i + 3]).astype(o_ref.dtype)


def _sb_prompt(zb, kt, vt, layer, batch, seq):
    nq = seq // MOBA_BLOCK
    wq = ATT_PAIRS * LANES
    cq = ZP_QB // wq
    kv_spec = pl.BlockSpec((None, None, wq, seq), lambda b, p, i: (layer, b, p, 0))
    return pl.pallas_call(
        _sb_body, grid=(batch, N_HEADS // 2 // ATT_PAIRS, nq),
        in_specs=[pl.BlockSpec((MOBA_BLOCK, wq), lambda b, p, i: (b * nq + i, cq + p)),
                  kv_spec, kv_spec],
        out_specs=pl.BlockSpec((MOBA_BLOCK, wq), lambda b, p, i: (b * nq + i, p)),
        out_shape=jax.ShapeDtypeStruct((batch * seq, W_ATT), BF16),
        compiler_params=_params(("parallel", "parallel", "arbitrary")),
        name="sb_prompt")(zb, kt, vt)


def _group_norm(v, g, b):
    hp = lax.Precision.HIGHEST
    gi = lax.broadcasted_iota(jnp.int32, (W_C, W_C), 0) // HEAD_DIM
    gj = lax.broadcasted_iota(jnp.int32, (W_C, W_C), 1) // HEAD_DIM
    gmean = jnp.where(gi == gj, 1.0 / HEAD_DIM, 0.0).astype(F32)
    mu = _dot(v, gmean, precision=hp)
    d = v - mu
    var = _dot(d * d, gmean, precision=hp)
    return d * lax.rsqrt(var + LN_EPS) * g + b


def _gmlp_prompt_body(u_ref, v_ref, ws_ref, bias_ref, g_ref, b_ref, o_ref, *, rows):
    ti = lax.broadcasted_iota(jnp.int32, (CHUNK, CHUNK), 0)
    tj = lax.broadcasted_iota(jnp.int32, (CHUNK, CHUNK), 1)
    lane = lax.broadcasted_iota(jnp.int32, (CHUNK, LANES), 1)
    wpair = []
    for pr in range(2):
        w0 = jnp.where(tj <= ti, ws_ref[2 * pr], 0.0)
        w1 = jnp.where(tj <= ti, ws_ref[2 * pr + 1], 0.0)
        wpair.append(jnp.concatenate([w0, w1], axis=1).astype(BF16))
    for r in range(rows // CHUNK):
        sl = slice(r * CHUNK, (r + 1) * CHUNK)
        u = jax.nn.gelu(u_ref[sl, :].astype(F32))
        vn = _group_norm(jax.nn.gelu(v_ref[sl, :].astype(F32)), g_ref[...], b_ref[...])
        mixes = []
        for pr in range(2):
            vp = vn[:, pr * LANES:(pr + 1) * LANES]
            rhs = jnp.concatenate([jnp.where(lane < HEAD_DIM, vp, 0.0),
                                   jnp.where(lane >= HEAD_DIM, vp, 0.0)], axis=0).astype(BF16)
            mixes.append(_dot(wpair[pr], rhs))
        mix = jnp.concatenate(mixes, axis=1) + bias_ref[...]
        o_ref[sl, :] = (u * mix).astype(o_ref.dtype)


def _gmlp_prompt(zb, ws, bias, g, b):
    n = zb.shape[0]
    rows = 512
    return pl.pallas_call(
        functools.partial(_gmlp_prompt_body, rows=rows), grid=(n // rows,),
        in_specs=[pl.BlockSpec((rows, W_C), lambda i: (i, ZP_UC // W_C)),
                  pl.BlockSpec((rows, W_C), lambda i: (i, ZP_VC // W_C)),
                  pl.BlockSpec((C_GROUPS, CHUNK, CHUNK), lambda i: (0, 0, 0)),
                  pl.BlockSpec((CHUNK, W_C), lambda i: (0, 0)),
                  pl.BlockSpec((1, W_C), lambda i: (0, 0)),
                  pl.BlockSpec((1, W_C), lambda i: (0, 0))],
        out_specs=pl.BlockSpec((rows, W_C), lambda i: (i, 0)),
        out_shape=jax.ShapeDtypeStruct((n, W_C), BF16),
        compiler_params=_params(("parallel",)), name="gmlp_prompt")(
            zb, zb, ws, bias, g.reshape(1, W_C), b.reshape(1, W_C))


def _gmlp_sample_body(u_ref, v_ref, coef_ref, bias_ref, g_ref, b_ref, o_ref, vn_ref, *, t_new, bsz):
    vns = []
    for t in range(t_new):
        sl = slice(t * bsz, (t + 1) * bsz)
        vn = _group_norm(jax.nn.gelu(v_ref[sl, :]), g_ref[...], b_ref[...])
        vn_ref[sl, :] = vn
        vns.append(vn)
    for t in range(t_new):
        sl = slice(t * bsz, (t + 1) * bsz)
        mix = bias_ref[t:t + 1, :]
        for s in range(t + 1):
            mix = mix + coef_ref[t * t_new + s:t * t_new + s + 1, :] * vns[s]
        o_ref[sl, :] = (jax.nn.gelu(u_ref[sl, :]) * mix).astype(o_ref.dtype)


def _gmlp_sample(zs, coef, bias, g, b, t_new, bsz):
    n = zs.shape[0]
    full = lambda shape: pl.BlockSpec(shape, lambda i: (0, 0))
    return pl.pallas_call(
        functools.partial(_gmlp_sample_body, t_new=t_new, bsz=bsz), grid=(1,),
        in_specs=[pl.BlockSpec((n, W_C), lambda i: (0, OFF_UC // W_C)),
                  pl.BlockSpec((n, W_C), lambda i: (0, OFF_VC // W_C)),
                  full((t_new * t_new, W_C)), full((t_new, W_C)), full((1, W_C)), full((1, W_C))],
        out_specs=[full((n, W_C)), full((n, W_C))],
        out_shape=[jax.ShapeDtypeStruct((n, W_C), BF16), jax.ShapeDtypeStruct((n, W_C), F32)],
        compiler_params=_params(("arbitrary",)), name="gmlp_sample")(
            zs, zs, coef, bias, g.reshape(1, W_C), b.reshape(1, W_C))


def _out_proj_body(ma_ref, mb_ref, mc_ref, h_ref, w_ref, g_ref, b_ref, o_ref, *, alpha):
    mixed = jnp.concatenate([ma_ref[...].astype(BF16), mb_ref[...].astype(BF16),
                             mc_ref[...].astype(BF16)], axis=1)
    y = _dot(mixed, w_ref[...])
    o_ref[...] = _ln(alpha * h_ref[...] + y, g_ref[...], b_ref[...])


def _out_proj(ma, mb, mc, h, w, g, b, alpha):
    n = h.shape[0]
    tm = min(512, n)
    row = lambda i: (i, 0)
    const = lambda i: (0, 0)
    return pl.pallas_call(
        functools.partial(_out_proj_body, alpha=alpha), grid=(n // tm,),
        in_specs=[pl.BlockSpec((tm, W_ATT), row), pl.BlockSpec((tm, W_ATT), row),
                  pl.BlockSpec((tm, W_C), row), pl.BlockSpec((tm, D_MODEL), row),
                  pl.BlockSpec((D_MODEL, D_MODEL), const),
                  pl.BlockSpec((1, D_MODEL), const), pl.BlockSpec((1, D_MODEL), const)],
        out_specs=pl.BlockSpec((tm, D_MODEL), row),
        out_shape=jax.ShapeDtypeStruct((n, D_MODEL), F32),
        compiler_params=_params(("parallel",)), name="out_proj")(
            ma, mb, mc, h, w, g.reshape(1, D_MODEL), b.reshape(1, D_MODEL))


def _ffn_body(h_ref, w1_ref, w2_ref, g_ref, b_ref, o_ref, acc_ref, *, alpha):
    h = h_ref[...]
    hb = h.astype(BF16)
    step = 512
    for c in range(0, D_FF, step):
        a = _dot(hb, w1_ref[:, c:c + step])
        a = jnp.square(jnp.maximum(a, 0.0)).astype(BF16)
        y = _dot(a, w2_ref[c:c + step, :])
        if c == 0:
            acc_ref[...] = y
        else:
            acc_ref[...] += y
    o_ref[...] = _ln(alpha * h + acc_ref[...], g_ref[...], b_ref[...])


def _ffn(h, w1, w2, g, b, alpha):
    n = h.shape[0]
    tm = min(512, n)
    row = lambda i: (i, 0)
    const = lambda i: (0, 0)
    return pl.pallas_call(
        functools.partial(_ffn_body, alpha=alpha), grid=(n // tm,),
        in_specs=[pl.BlockSpec((tm, D_MODEL), row),
                  pl.BlockSpec((D_MODEL, D_FF), const, pipeline_mode=pl.Buffered(1)),
                  pl.BlockSpec((D_FF, D_MODEL), const, pipeline_mode=pl.Buffered(1)),
                  pl.BlockSpec((1, D_MODEL), const), pl.BlockSpec((1, D_MODEL), const)],
        out_specs=pl.BlockSpec((tm, D_MODEL), row),
        out_shape=jax.ShapeDtypeStruct((n, D_MODEL), F32),
        scratch_shapes=[pltpu.VMEM((tm, D_MODEL), F32)],
        compiler_params=_params(("parallel",)), name="ffn")(
            h, w1, w2, g.reshape(1, D_MODEL), b.reshape(1, D_MODEL))


PAGES_PER_BLK = MOBA_BLOCK // PAGE
A_GROUP = 2
A_RING = 3
B_SLOTS = 3


def _expand_heads(q):
    r = lax.broadcasted_iota(jnp.int32, (8, W_ATT), 0)
    c = lax.broadcasted_iota(jnp.int32, (8, W_ATT), 1) // HEAD_DIM
    pieces = [jnp.where(r == c, jnp.broadcast_to(row, (8, W_ATT)), 0.0) for row in q]
    return jnp.concatenate(pieces, axis=0)


def _reduce_heads(o, out_ref, b):
    r = lax.broadcasted_iota(jnp.int32, (8, W_ATT), 0)
    c = lax.broadcasted_iota(jnp.int32, (8, W_ATT), 1) // HEAD_DIM
    for t in range(o.shape[0] // 8):
        piece = jnp.where(r == c, o[8 * t:8 * t + 8, :], 0.0)
        out_ref[t, pl.ds(b, 1), :] = jnp.sum(piece, axis=0, keepdims=True)


def _sample_attn_body(pt_ref, zs_ref, cka, cva, ckb, cvb, oa_ref, ob_ref,
                      ka_buf, va_buf, kb_buf, vb_buf, sem_a, sem_b,
                      m_s, l_s, acc_s, gate_s, carry_s, accb_s, live_s,
                      *, layer, bsz, n_pages, t_new, slopes):
    b = pl.program_id(0)
    past_len = n_pages * PAGE
    n_blk = n_pages // PAGES_PER_BLK
    n_grp = n_blk // A_GROUP
    total_grp = bsz * n_grp

    def blk_copies(src_k, src_v, dst_k, dst_v, sem, first_page, slot):
        cps = []
        for k in range(PAGES_PER_BLK):
            pg = pt_ref[first_page + k]
            win = pl.ds(k * PAGE, PAGE)
            cps.append(pltpu.make_async_copy(src_k.at[layer, pg], dst_k.at[slot, :, win],
                                             sem.at[slot, 2 * k]))
            cps.append(pltpu.make_async_copy(src_v.at[layer, pg], dst_v.at[slot, :, win],
                                             sem.at[slot, 2 * k + 1]))
        return cps

    def a_copies(fg, k):
        return blk_copies(cka, cva, ka_buf, va_buf, sem_a,
                          (fg * A_GROUP + k) * PAGES_PER_BLK, (fg % A_RING) * A_GROUP + k)

    def b_copies(seq, back, slot):
        return blk_copies(ckb, cvb, kb_buf, vb_buf, sem_b,
                          seq * n_pages + (n_blk - 1 - back) * PAGES_PER_BLK, slot)

    @pl.when(b == 0)
    def _prime():
        for fg in range(A_RING - 1):
            for k in range(A_GROUP):
                for cp in a_copies(fg, k):
                    cp.start()
        for cp in b_copies(0, 0, 0):
            cp.start()

    @pl.when(b + 1 < bsz)
    def _prefetch_next_sb():
        for cp in b_copies(b + 1, 0, (b + 1) % 2):
            cp.start()

    scale = HEAD_DIM ** -0.5
    rows = 8 * t_new
    ri = lax.broadcasted_iota(jnp.int32, (rows, 1), 0)
    tok = ri // 8
    hd = ri % 8
    slope = jnp.zeros((rows, 1), F32)
    for h in range(N_HEADS):
        slope = jnp.where(hd == h, float(slopes[h]), slope)

    def seg(t, off):
        return zs_ref[t, pl.ds(b, 1), off:off + W_ATT]

    qa = _expand_heads([seg(t, OFF_QA) * scale for t in range(t_new)])
    qb = _expand_heads([seg(t, OFF_QB) * scale for t in range(t_new)])
    upper2 = _upper2(MOBA_BLOCK)

    sa, ys = [], []
    for s in range(t_new):
        sc = jnp.sum(qa * seg(s, OFF_KA), axis=1, keepdims=True)
        sc = sc - slope * (tok - s).astype(F32)
        sa.append(jnp.where(tok >= s, sc, -jnp.inf))
        ys.append(jnp.sum(qb * seg(s, OFF_KB), axis=1, keepdims=True) * LOG2E)
    m = sa[0]
    for s in range(1, t_new):
        m = jnp.maximum(m, sa[s])
    l = jnp.zeros((rows, 1), F32)
    acc = jnp.zeros((rows, W_ATT), F32)
    for s in range(t_new):
        pr = jnp.exp(sa[s] - m)
        l = l + pr
        acc = acc + pr * seg(s, OFF_VA)
    m_s[n_blk] = m
    l_s[n_blk] = l
    acc_s[n_blk] = acc
    carry = jnp.zeros((rows, 1), F32)
    accb = jnp.zeros((rows, W_ATT), F32)
    for s in range(t_new - 1, -1, -1):
        valid = tok > s
        sp = _softplus2(ys[s])
        a = jnp.where(valid, jnp.exp2(ys[s] - sp + carry), 0.0)
        accb = accb + a * seg(s, OFF_VB)
        carry = carry - jnp.where(valid, sp, 0.0)
    carry_s[...] = carry
    accb_s[...] = accb
    live_s[0] = (jnp.max(carry) > SB_EXIT2).astype(jnp.int32)

    qa_b = qa.astype(BF16)
    qb_b = qb.astype(BF16)
    kcol = lax.broadcasted_iota(jnp.int32, (rows, MOBA_BLOCK), 1)

    def moba_group(gi, _):
        fg = b * n_grp + gi
        for k in range(A_GROUP):
            for cp in a_copies(fg, k):
                cp.wait()

        @pl.when(fg + A_RING - 1 < total_grp)
        def _start_ahead():
            for k in range(A_GROUP):
                for cp in a_copies(fg + A_RING - 1, k):
                    cp.start()

        for k in range(A_GROUP):
            slot = (fg % A_RING) * A_GROUP + k
            blk = gi * A_GROUP + k
            raw = _dot(qa_b, ka_buf[slot].astype(BF16))
            gate_s[blk] = jnp.sum(raw, axis=1, keepdims=True)
            dist = (past_len + tok - blk * MOBA_BLOCK - kcol).astype(F32)
            sc = raw - slope * dist
            m = jnp.max(sc, axis=1, keepdims=True)
            pr = jnp.exp(sc - m)
            m_s[blk] = m
            l_s[blk] = jnp.sum(pr, axis=1, keepdims=True)
            acc_s[blk] = _dot_nt(pr.astype(BF16), va_buf[slot].astype(BF16))
        return 0

    lax.fori_loop(0, n_grp, moba_group, 0)

    def sb_block(slot):
        y = _dot(qb_b, kb_buf[slot].astype(BF16)) * LOG2E
        sp = _softplus2(y)
        lk = -sp
        aft = _suffix_sums(lk, upper2)
        carry = carry_s[...]
        a = jnp.exp2(y - sp + aft + carry)
        accb_s[...] = accb_s[...] + _dot_nt(a.astype(BF16), vb_buf[slot].astype(BF16))
        carry = carry + aft[:, 0:1] + lk[:, 0:1]
        carry_s[...] = carry
        live_s[0] = (jnp.max(carry) > SB_EXIT2).astype(jnp.int32)

    for cp in b_copies(b, 0, b % 2):
        cp.wait()

    @pl.when(live_s[0] > 0)
    def _recent():
        sb_block(b % 2)

    def older_cond(c):
        return (c[0] < n_blk) & (c[1] > 0)

    def older_body(c):
        cps = b_copies(b, c[0], B_SLOTS - 1)
        for cp in cps:
            cp.start()
        for cp in cps:
            cp.wait()
        sb_block(B_SLOTS - 1)
        return c[0] + 1, live_s[0]

    lax.while_loop(older_cond, older_body, (jnp.int32(1), live_s[0]))

    gates = [gate_s[nb] for nb in range(n_blk)]
    sel = []
    for nb in range(n_blk):
        rank = jnp.zeros((rows, 1), jnp.int32)
        for mth in range(n_blk):
            if mth < nb:
                rank += jnp.where(gates[mth] >= gates[nb], 1, 0)
            elif mth > nb:
                rank += jnp.where(gates[mth] > gates[nb], 1, 0)
        sel.append(rank < MOBA_TOPK)
    big = m_s[n_blk]
    for nb in range(n_blk):
        big = jnp.maximum(big, jnp.where(sel[nb], m_s[nb], -jnp.inf))
    w = jnp.exp(m_s[n_blk] - big)
    l = w * l_s[n_blk]
    acc = w * acc_s[n_blk]
    for nb in range(n_blk):
        w = jnp.where(sel[nb], jnp.exp(jnp.minimum(m_s[nb] - big, 0.0)), 0.0)
        l = l + w * l_s[nb]
        acc = acc + w * acc_s[nb]
    _reduce_heads(acc / l, oa_ref, b)
    _reduce_heads(accb_s[...], ob_ref, b)


def _sample_attn(zs3, caches, layer, page_table, slopes):
    t_new, bsz, _ = zs3.shape
    n_pages = page_table.shape[1]
    n_blk = n_pages // PAGES_PER_BLK
    assert n_pages % (PAGES_PER_BLK * A_GROUP) == 0 and bsz * n_blk // A_GROUP >= A_RING - 1
    rows = 8 * t_new
    hbm = pl.BlockSpec(memory_space=pl.ANY)
    blk_buf = lambda n: pltpu.VMEM((n, W_ATT, MOBA_BLOCK), F32)
    out_spec = pl.BlockSpec((t_new, bsz, W_ATT), lambda b, pt: (0, 0, 0))
    grid_spec = pltpu.PrefetchScalarGridSpec(
        num_scalar_prefetch=1, grid=(bsz,),
        in_specs=[pl.BlockSpec((t_new, bsz, IN_WIDTH), lambda b, pt: (0, 0, 0)),
                  hbm, hbm, hbm, hbm],
        out_specs=[out_spec, out_spec],
        scratch_shapes=[blk_buf(A_RING * A_GROUP), blk_buf(A_RING * A_GROUP),
                        blk_buf(B_SLOTS), blk_buf(B_SLOTS),
                        pltpu.SemaphoreType.DMA((A_RING * A_GROUP, 2 * PAGES_PER_BLK)),
                        pltpu.SemaphoreType.DMA((B_SLOTS, 2 * PAGES_PER_BLK)),
                        pltpu.VMEM((n_blk + 1, rows, 1), F32),
                        pltpu.VMEM((n_blk + 1, rows, 1), F32),
                        pltpu.VMEM((n_blk + 1, rows, W_ATT), F32),
                        pltpu.VMEM((n_blk, rows, 1), F32),
                        pltpu.VMEM((rows, 1), F32),
                        pltpu.VMEM((rows, W_ATT), F32),
                        pltpu.SMEM((1,), jnp.int32)])
    body = functools.partial(_sample_attn_body, layer=layer, bsz=bsz, n_pages=n_pages,
                             t_new=t_new, slopes=tuple(float(s) for s in slopes))
    return pl.pallas_call(
        body, grid_spec=grid_spec,
        out_shape=[jax.ShapeDtypeStruct((t_new, bsz, W_ATT), F32)] * 2,
        compiler_params=_params(("arbitrary",)), name="sample_attn")(
            page_table.reshape(-1), zs3, *caches)


def _gate_matrix(kmean_t):
    rows = jnp.arange(W_ATT)[:, None] // HEAD_DIM
    cols = jnp.arange(LANES)[None, :] // 16
    tiled = jnp.tile(kmean_t[:, :, :16], (1, 1, LANES // 16))
    return jnp.where(rows == cols, tiled, 0.0)


def kernel(x_prompt, x_sample, cache_k_a, cache_v_a, cache_k_b, cache_v_b, page_table, ln_in_g, ln_in_b, w_in, w_out, w_spatial, b_spatial, ln_c_g, ln_c_b, ln1_g, ln1_b, w_ff1, w_ff2, ln2_g, ln2_b):
    batch, seq, _ = x_prompt.shape
    dec_batch, dec_seq, _ = x_sample.shape
    depth = w_in.shape[0]
    alpha = (2 * depth) ** 0.25
    slopes_np = _alibi_slopes(N_HEADS)
    slopes = jnp.asarray(slopes_np)
    n_blk = seq // MOBA_BLOCK
    assert n_blk <= 16 and seq % 512 == 0 and dec_seq <= CHUNK and dec_batch % 8 == 0
    assert all(math.frexp(float(s))[0] == 0.5 for s in slopes_np)

    w_in_b = w_in.astype(BF16)
    w_q = jnp.concatenate([w_in_b[:, :, OFF_QA:OFF_QA + W_ATT], w_in_b[:, :, OFF_QB:OFF_QB + W_ATT],
                           w_in_b[:, :, OFF_UC:]], axis=2)
    w_kv_t = jnp.swapaxes(jnp.concatenate(
        [w_in_b[:, :, OFF_KA:OFF_KA + 2 * W_ATT], w_in_b[:, :, OFF_KB:OFF_KB + 2 * W_ATT]],
        axis=2), 1, 2)
    w_out_b = w_out.astype(BF16)
    w_ff1_b = w_ff1.astype(BF16)
    w_ff2_b = w_ff2.astype(BF16)
    caches = [jnp.transpose(c, (0, 1, 3, 4, 2)).reshape(c.shape[0], c.shape[1], W_ATT, PAGE)
              for c in (cache_k_a, cache_v_a, cache_k_b, cache_v_b)]

    bias_p = jnp.repeat(jnp.swapaxes(b_spatial, 1, 2), HEAD_DIM, axis=2)
    coef_s = jnp.repeat(jnp.transpose(w_spatial[:, :, :dec_seq, :dec_seq], (0, 2, 3, 1)),
                        HEAD_DIM, axis=3).reshape(depth, dec_seq * dec_seq, W_C)
    bias_s = jnp.repeat(jnp.swapaxes(b_spatial[:, :, :dec_seq], 1, 2), HEAD_DIM, axis=2)

    hp = x_prompt.reshape(batch * seq, D_MODEL)
    hs = jnp.swapaxes(x_sample, 0, 1).reshape(dec_seq * dec_batch, D_MODEL)
    kv_p = [jnp.zeros((depth, batch, W_ATT, seq), F32) for _ in range(4)]
    kv_s = [jnp.zeros((depth, dec_seq, W_ATT, dec_batch), F32) for _ in range(4)]
    vc_s = []
    for l in range(depth):
        ln = (ln_in_g, ln_in_b) if l == 0 else None
        lnc_g, lnc_b = ln_c_g[l].reshape(-1), ln_c_b[l].reshape(-1)

        hp, zb, kv_p, kmean_t = _in_proj_prompt(hp, w_q[l], w_kv_t[l], ln, kv_p, l, depth, batch, seq)
        selb = _moba_gate(zb, _gate_matrix(kmean_t), batch, seq)
        mix_a = _moba_prompt(zb, kv_p[0], kv_p[1], selb, slopes, l, batch, seq)
        mix_b = _sb_prompt(zb, kv_p[2], kv_p[3], l, batch, seq)
        mix_c = _gmlp_prompt(zb, w_spatial[l], bias_p[l], lnc_g, lnc_b)
        h1 = _out_proj(mix_a, mix_b, mix_c, hp, w_out_b[l], ln1_g[l], ln1_b[l], alpha)
        hp = _ffn(h1, w_ff1_b[l], w_ff2_b[l], ln2_g[l], ln2_b[l], alpha)

        hs, zs, kv_s = _in_proj_sample(hs, w_in_b[l], w_kv_t[l], ln, kv_s, l, depth, dec_seq, dec_batch)
        oa, ob = _sample_attn(zs.reshape(dec_seq, dec_batch, IN_WIDTH), caches, l,
                              page_table, slopes_np)
        mix_c, vn = _gmlp_sample(zs, coef_s[l], bias_s[l], lnc_g, lnc_b, dec_seq, dec_batch)
        vc_s.append(vn)
        h1 = _out_proj(oa.reshape(-1, W_ATT), ob.reshape(-1, W_ATT), mix_c, hs,
                       w_out_b[l], ln1_g[l], ln1_b[l], alpha)
        hs = _ffn(h1, w_ff1_b[l], w_ff2_b[l], ln2_g[l], ln2_b[l], alpha)

    kv_out_p = [jnp.transpose(a.reshape(depth, batch, N_HEADS, HEAD_DIM, seq), (0, 1, 4, 2, 3))
                for a in kv_p]
    kv_out_s = [jnp.transpose(a.reshape(depth, dec_seq, N_HEADS, HEAD_DIM, dec_batch), (0, 4, 1, 2, 3))
                for a in kv_s]
    vc_out = jnp.swapaxes(jnp.stack(vc_s).reshape(depth, dec_seq, dec_batch, W_C), 1, 2)
    y_s = jnp.swapaxes(hs.reshape(dec_seq, dec_batch, D_MODEL), 0, 1)
    return (hp.reshape(batch, seq, D_MODEL), y_s, *kv_out_p, *kv_out_s, vc_out)
```

```python
import functools
import math

import jax
import jax.numpy as jnp
import numpy as np
from jax import lax
from jax.experimental import pallas as pl
from jax.experimental.pallas import tpu as pltpu

F32 = jnp.float32
BF16 = jnp.bfloat16

D_MODEL = 1024
HEAD_DIM = 64
N_HEADS = 6
W_ATT = N_HEADS * HEAD_DIM
C_GROUPS = 4
W_C = C_GROUPS * HEAD_DIM
IN_WIDTH = 6 * W_ATT + 2 * W_C
D_FF = 4 * D_MODEL
MOBA_BLOCK = 256
MOBA_TOPK = 3
CHUNK = 128
PAGE = 128
LN_EPS = 1e-5
LANES = 128
NEG_BIG = -1e30
SB_EXIT = -110.0

OFF_QA, OFF_KA, OFF_VA = 0, W_ATT, 2 * W_ATT
OFF_QB, OFF_KB, OFF_VB = 3 * W_ATT, 4 * W_ATT, 5 * W_ATT
OFF_UC, OFF_VC = 6 * W_ATT, 6 * W_ATT + W_C
ZP_QA, ZP_QB, ZP_UC, ZP_VC = 0, W_ATT, 2 * W_ATT, 2 * W_ATT + W_C
ZP_WIDTH = 2 * W_ATT + 2 * W_C

VMEM_LIMIT = 56 * 1024 * 1024


def _alibi_slopes(n):
    def pow2(m):
        start = 2.0 ** (-8.0 / m)
        return [start ** (i + 1) for i in range(m)]
    p = 2 ** int(math.floor(math.log2(n)))
    s = pow2(p)
    if p < n:
        s = s + pow2(2 * p)[0::2][: n - p]
    return np.array(s, dtype=np.float32)


def _ln(x, g, b):
    mu = jnp.mean(x, axis=-1, keepdims=True)
    d = x - mu
    var = jnp.mean(d * d, axis=-1, keepdims=True)
    return d * lax.rsqrt(var + LN_EPS) * g + b


def _dot(a, b, precision=None):
    return jnp.dot(a, b, preferred_element_type=F32, precision=precision)


def _dot_nt(a, b, precision=None):
    return lax.dot_general(a, b, (((1,), (1,)), ((), ())),
                           preferred_element_type=F32, precision=precision)


def _params(sem):
    return pltpu.CompilerParams(dimension_semantics=sem, vmem_limit_bytes=VMEM_LIMIT)


def _in_proj_prompt_body(*refs, apply_ln, tm):
    it = iter(refs)
    x_ref = next(it)
    if apply_ln:
        g_ref, b_ref = next(it), next(it)
    wq_ref, wkv_ref = next(it), next(it)
    for _ in range(4):
        next(it)
    if apply_ln:
        h_ref = next(it)
    zb_ref = next(it)
    kv_refs = [next(it) for _ in range(4)]
    km_ref = next(it)
    ti = pl.program_id(1)

    x = x_ref[...]
    if apply_ln:
        x = _ln(x, g_ref[...], b_ref[...])
        h_ref[...] = x
    hb = x.astype(BF16)
    for c in range(0, ZP_WIDTH, 256):
        zb_ref[:, c:c + 256] = _dot(hb, wq_ref[:, c:c + 256]).astype(BF16)
    lane = lax.broadcasted_iota(jnp.int32, (W_ATT, LANES), 1)

    @pl.when(ti == 0)
    def _zero_means():
        km_ref[...] = jnp.zeros_like(km_ref)

    for t in range(4):
        kt = _dot_nt(wkv_ref[t * W_ATT:(t + 1) * W_ATT, :], hb)
        kv_refs[t][...] = kt
        if t == 0:
            km = km_ref[...]
            for r in range(tm // MOBA_BLOCK):
                mean = jnp.sum(kt[:, r * MOBA_BLOCK:(r + 1) * MOBA_BLOCK], axis=1,
                               keepdims=True) * (1.0 / MOBA_BLOCK)
                km = jnp.where(lane == ti * (tm // MOBA_BLOCK) + r, mean, km)
            km_ref[...] = km


def _in_proj_prompt(x, wq, wkv, ln, kv_prev, layer, depth, batch, seq):
    n = x.shape[0]
    tm = 512
    nt = seq // tm
    row = lambda b, t: (b * nt + t, 0)
    const = lambda b, t: (0, 0)
    args, in_specs = [x], [pl.BlockSpec((tm, D_MODEL), row)]
    if ln is not None:
        args += [ln[0].reshape(1, D_MODEL), ln[1].reshape(1, D_MODEL)]
        in_specs += [pl.BlockSpec((1, D_MODEL), const)] * 2
    args += [wq, wkv]
    in_specs += [pl.BlockSpec((D_MODEL, ZP_WIDTH), const), pl.BlockSpec((4 * W_ATT, D_MODEL), const)]
    out_shape, out_specs = [], []
    if ln is not None:
        out_shape.append(jax.ShapeDtypeStruct((n, D_MODEL), F32))
        out_specs.append(pl.BlockSpec((tm, D_MODEL), row))
    out_shape.append(jax.ShapeDtypeStruct((n, ZP_WIDTH), BF16))
    out_specs.append(pl.BlockSpec((tm, ZP_WIDTH), row))
    aliases = {}
    for t in range(4):
        aliases[len(args)] = len(out_shape)
        args.append(kv_prev[t])
        in_specs.append(pl.BlockSpec(memory_space=pl.ANY))
        out_shape.append(jax.ShapeDtypeStruct((depth, batch, W_ATT, seq), F32))
        out_specs.append(pl.BlockSpec((None, None, W_ATT, tm), lambda b, t: (layer, b, 0, t)))
    out_shape.append(jax.ShapeDtypeStruct((batch, W_ATT, LANES), F32))
    out_specs.append(pl.BlockSpec((None, W_ATT, LANES), lambda b, t: (b, 0, 0)))
    body = functools.partial(_in_proj_prompt_body, apply_ln=ln is not None, tm=tm)
    outs = list(pl.pallas_call(
        body, grid=(batch, nt), in_specs=in_specs, out_specs=out_specs, out_shape=out_shape,
        input_output_aliases=aliases, compiler_params=_params(("parallel", "arbitrary")),
        name="in_proj_prompt")(*args))
    h = outs.pop(0) if ln is not None else x
    return h, outs[0], outs[1:5], outs[5]


def _in_proj_sample_body(*refs, apply_ln, t_new, bsz):
    it = iter(refs)
    x_ref = next(it)
    if apply_ln:
        g_ref, b_ref = next(it), next(it)
    w_ref, wkv_ref = next(it), next(it)
    for _ in range(4):
        next(it)
    if apply_ln:
        h_ref = next(it)
    zs_ref = next(it)
    kv_refs = [next(it) for _ in range(4)]

    x = x_ref[...]
    if apply_ln:
        x = _ln(x, g_ref[...], b_ref[...])
        h_ref[...] = x
    hb = x.astype(BF16)
    for c in range(0, IN_WIDTH, 256):
        zs_ref[:, c:c + 256] = _dot(hb, w_ref[:, c:c + 256])
    for t in range(4):
        kt = _dot_nt(wkv_ref[t * W_ATT:(t + 1) * W_ATT, :], hb)
        for tt in range(t_new):
            kv_refs[t][tt] = kt[:, tt * bsz:(tt + 1) * bsz]


def _in_proj_sample(x, w, wkv, ln, kv_prev, layer, depth, t_new, bsz):
    n = x.shape[0]
    const = lambda i: (0, 0)
    args, in_specs = [x], [pl.BlockSpec((n, D_MODEL), const)]
    if ln is not None:
        args += [ln[0].reshape(1, D_MODEL), ln[1].reshape(1, D_MODEL)]
        in_specs += [pl.BlockSpec((1, D_MODEL), const)] * 2
    args += [w, wkv]
    in_specs += [pl.BlockSpec((D_MODEL, IN_WIDTH), const), pl.BlockSpec((4 * W_ATT, D_MODEL), const)]
    out_shape, out_specs = [], []
    if ln is not None:
        out_shape.append(jax.ShapeDtypeStruct((n, D_MODEL), F32))
        out_specs.append(pl.BlockSpec((n, D_MODEL), const))
    out_shape.append(jax.ShapeDtypeStruct((n, IN_WIDTH), F32))
    out_specs.append(pl.BlockSpec((n, IN_WIDTH), const))
    aliases = {}
    for t in range(4):
        aliases[len(args)] = len(out_shape)
        args.append(kv_prev[t])
        in_specs.append(pl.BlockSpec(memory_space=pl.ANY))
        out_shape.append(jax.ShapeDtypeStruct((depth, t_new, W_ATT, bsz), F32))
        out_specs.append(pl.BlockSpec((None, t_new, W_ATT, bsz), lambda i: (layer, 0, 0, 0)))
    body = functools.partial(_in_proj_sample_body, apply_ln=ln is not None, t_new=t_new, bsz=bsz)
    outs = list(pl.pallas_call(
        body, grid=(1,), in_specs=in_specs, out_specs=out_specs, out_shape=out_shape,
        input_output_aliases=aliases, compiler_params=_params(("arbitrary",)),
        name="in_proj_sample")(*args))
    h = outs.pop(0) if ln is not None else x
    return h, outs[0], outs[1:5]


def _gate_body(q_ref, km_ref, o_ref):
    cur = pl.program_id(1)
    g = _dot(q_ref[...].astype(F32), km_ref[...], precision=lax.Precision.HIGHEST)
    blk = lax.broadcasted_iota(jnp.int32, g.shape, 1) & 15
    past = blk < cur
    g = jnp.where(past, g, -jnp.inf)
    rank = jnp.zeros(g.shape, jnp.int32)
    for s in range(1, 16):
        lower = pltpu.roll(g, s, axis=1)
        rank += jnp.where((blk >= s) & (lower >= g), 1, 0)
        upper = pltpu.roll(g, LANES - s, axis=1)
        rank += jnp.where((blk + s <= 15) & (upper > g), 1, 0)
    o_ref[...] = jnp.where(past & (rank < MOBA_TOPK), 0.0, NEG_BIG)


def _moba_gate(zb, km, batch, seq):
    nq = seq // MOBA_BLOCK
    return pl.pallas_call(
        _gate_body, grid=(batch, nq),
        in_specs=[pl.BlockSpec((MOBA_BLOCK, W_ATT), lambda b, i: (b * nq + i, ZP_QA // W_ATT)),
                  pl.BlockSpec((None, W_ATT, LANES), lambda b, i: (b, 0, 0))],
        out_specs=pl.BlockSpec((MOBA_BLOCK, LANES), lambda b, i: (b * nq + i, 0)),
        out_shape=jax.ShapeDtypeStruct((batch * seq, LANES), F32),
        compiler_params=_params(("parallel", "parallel")), name="moba_gate")(zb, km)


ROLE_SEL0, ROLE_SEL1 = 0, 1
ROLE_ROW, ROLE_KEY = 96, 97
ROLE_BLK, ROLE_2ND = 98, 99


ATT_PAIRS = 3


def _moba_body(slopes_ref, q_ref, kt_ref, vt_ref, selb_ref, o_ref, lhs_s, rhs_s):
    pg = pl.program_id(1)
    i = pl.program_id(2)
    tq = MOBA_BLOCK
    selb = selb_ref[...]
    lane = lax.broadcasted_iota(jnp.int32, (tq, LANES), 1)
    rowf = lax.broadcasted_iota(jnp.int32, (tq, LANES), 0).astype(F32)
    role = lax.broadcasted_iota(jnp.int32, (LANES, 2 * tq), 0)
    key2 = lax.broadcasted_iota(jnp.int32, (LANES, 2 * tq), 1)
    second = key2 >= tq
    keyf = jnp.where(second, key2 - tq, key2).astype(F32)
    rhs_roles = jnp.where(
        (role == ROLE_SEL0) & jnp.logical_not(second), 1.0,
        jnp.where(((role == ROLE_SEL1) | (role == ROLE_2ND)) & second, 1.0,
                  jnp.where((role == ROLE_ROW) | (role == ROLE_BLK), 1.0,
                            jnp.where(role == ROLE_KEY, keyf, 0.0)))).astype(BF16)
    rq = lax.broadcasted_iota(jnp.int32, (tq, tq), 0)
    ck = lax.broadcasted_iota(jnp.int32, (tq, tq), 1)
    d0 = (rq - ck).astype(F32)
    causal = ck <= rq
    row0 = pl.multiple_of(i * tq, tq)
    fixed = (lane == ROLE_ROW) | (lane == ROLE_KEY) | (lane == ROLE_2ND)

    heads, state = [], []
    for pr_i in range(ATT_PAIRS):
        lanes = slice(pr_i * LANES, (pr_i + 1) * LANES)
        rhs_s[pr_i, LANES:2 * LANES, :] = rhs_roles
        q = q_ref[:, lanes].astype(F32)
        kd = kt_ref[lanes, pl.ds(row0, tq)].astype(BF16)
        vd = vt_ref[lanes, pl.ds(row0, tq)].astype(BF16)
        for hh in range(2):
            hi = 2 * pr_i + hh
            head = 2 * (pg * ATT_PAIRS + pr_i) + hh
            slope = slopes_ref[head]
            in_head = (lane >= HEAD_DIM * hh) & (lane < HEAD_DIM * (hh + 1))
            qh = jnp.where(in_head, q * (HEAD_DIM ** -0.5), 0.0).astype(BF16)
            lhs_s[hi, :, 0:LANES] = qh
            lhs_fix = jnp.where(lane == ROLE_ROW, -slope * rowf,
                                jnp.where(lane == ROLE_KEY, slope, slope * tq))
            heads.append((head, slope, lhs_fix))
            s = jnp.where(causal, _dot(qh, kd) - slope * d0, -jnp.inf)
            m = jnp.max(s, axis=1, keepdims=True)
            pr = jnp.exp(s - m)
            state += [m, jnp.sum(pr, axis=1, keepdims=True), _dot_nt(pr.astype(BF16), vd)]

    def body(pp, carry):
        j0 = 2 * pp
        r0 = pl.multiple_of(j0 * tq, 2 * tq)
        out = []
        for pr_i in range(ATT_PAIRS):
            lanes = slice(pr_i * LANES, (pr_i + 1) * LANES)
            rhs_s[pr_i, 0:LANES, :] = kt_ref[lanes, pl.ds(r0, 2 * tq)].astype(BF16)
            v2 = vt_ref[lanes, pl.ds(r0, 2 * tq)].astype(BF16)
            for hh in range(2):
                hi = 2 * pr_i + hh
                head, slope, lhs_fix = heads[hi]
                m, l, acc = carry[3 * hi:3 * hi + 3]
                sel = pltpu.roll(selb, (LANES - head * 16 - j0) % LANES, axis=1)
                off = -slope * ((i - j0) * tq).astype(F32)
                ext = jnp.where(fixed, lhs_fix, jnp.where(lane == ROLE_BLK, off, sel))
                lhs_s[hi, :, LANES:2 * LANES] = ext.astype(BF16)
                t = _dot(lhs_s[hi], rhs_s[pr_i])
                m_new = jnp.maximum(m, jnp.max(t, axis=1, keepdims=True))
                alpha = jnp.exp(m - m_new)
                pr = jnp.exp(t - m_new)
                l = alpha * l + jnp.sum(pr, axis=1, keepdims=True)
                acc = alpha * acc + _dot_nt(pr.astype(BF16), v2)
                out += [m_new, l, acc]
        return tuple(out)

    state = lax.fori_loop(0, (i + 1) // 2, body, tuple(state))
    for pr_i in range(ATT_PAIRS):
        o0 = state[6 * pr_i + 2] / state[6 * pr_i + 1]
        o1 = state[6 * pr_i + 5] / state[6 * pr_i + 4]
        o_ref[:, pr_i * LANES:(pr_i + 1) * LANES] = jnp.where(lane < HEAD_DIM, o0, o1).astype(o_ref.dtype)


def _moba_prompt(zb, kt, vt, selb, slopes, layer, batch, seq):
    nq = seq // MOBA_BLOCK
    wq = ATT_PAIRS * LANES
    cq = ZP_QA // wq
    kv_spec = pl.BlockSpec((None, None, wq, seq), lambda b, p, i, s: (layer, b, p, 0))
    grid_spec = pltpu.PrefetchScalarGridSpec(
        num_scalar_prefetch=1, grid=(batch, N_HEADS // 2 // ATT_PAIRS, nq),
        in_specs=[pl.BlockSpec((MOBA_BLOCK, wq), lambda b, p, i, s: (b * nq + i, cq + p)),
                  kv_spec, kv_spec,
                  pl.BlockSpec((MOBA_BLOCK, LANES), lambda b, p, i, s: (b * nq + i, 0))],
        out_specs=pl.BlockSpec((MOBA_BLOCK, wq), lambda b, p, i, s: (b * nq + i, p)),
        scratch_shapes=[pltpu.VMEM((2 * ATT_PAIRS, MOBA_BLOCK, 2 * LANES), BF16),
                        pltpu.VMEM((ATT_PAIRS, 2 * LANES, 2 * MOBA_BLOCK), BF16)])
    return pl.pallas_call(
        _moba_body, grid_spec=grid_spec,
        out_shape=jax.ShapeDtypeStruct((batch * seq, W_ATT), BF16),
        compiler_params=_params(("parallel", "parallel", "arbitrary")),
        name="moba_prompt")(slopes, zb, kt, vt, selb)


LOG2E = 1.4426950408889634
SB_EXIT2 = SB_EXIT * LOG2E


def _softplus2(y):
    return jnp.maximum(y, 0.0) + jnp.log2(1.0 + jnp.exp2(-jnp.abs(y)))


def _suffix_sums(lk, upper2):
    hi = lk.astype(BF16)
    lo = (lk - hi.astype(F32)).astype(BF16)
    return _dot(jnp.concatenate([hi, lo], axis=1), upper2)


def _upper2(n):
    r = lax.broadcasted_iota(jnp.int32, (2 * n, n), 0)
    c = lax.broadcasted_iota(jnp.int32, (2 * n, n), 1)
    return jnp.where(jnp.where(r >= n, r - n, r) > c, 1.0, 0.0).astype(BF16)


def _sb_body(q_ref, kt_ref, vt_ref, o_ref):
    i = pl.program_id(2)
    tq = MOBA_BLOCK
    lane = lax.broadcasted_iota(jnp.int32, (tq, LANES), 1)
    rq = lax.broadcasted_iota(jnp.int32, (tq, tq), 0)
    ck = lax.broadcasted_iota(jnp.int32, (tq, tq), 1)
    before = ck < rq
    upper2 = _upper2(tq)
    n_heads = 2 * ATT_PAIRS
    qhs = []
    for pr_i in range(ATT_PAIRS):
        q = q_ref[:, pr_i * LANES:(pr_i + 1) * LANES].astype(F32)
        for hh in range(2):
            in_head = (lane >= HEAD_DIM * hh) & (lane < HEAD_DIM * (hh + 1))
            qhs.append(jnp.where(in_head, q * (HEAD_DIM ** -0.5), 0.0).astype(BF16))

    def block(j, st, diag):
        r0 = pl.multiple_of(j * tq, tq)
        out = []
        for pr_i in range(ATT_PAIRS):
            lanes = slice(pr_i * LANES, (pr_i + 1) * LANES)
            kb = kt_ref[lanes, pl.ds(r0, tq)].astype(BF16)
            vb = vt_ref[lanes, pl.ds(r0, tq)].astype(BF16)
            for hh in range(2):
                hi = 2 * pr_i + hh
                carry, acc = st[2 * hi], st[2 * hi + 1]
                y = _dot(qhs[hi], kb) * LOG2E
                sp = _softplus2(y)
                lk = -sp
                if diag:
                    lk = jnp.where(before, lk, 0.0)
                aft = _suffix_sums(lk, upper2)
                a = jnp.exp2(y - sp + aft + carry)
                if diag:
                    a = jnp.where(before, a, 0.0)
                out += [carry + aft[:, 0:1] + lk[:, 0:1], acc + _dot_nt(a.astype(BF16), vb)]
        return tuple(out)

    def alive(st):
        top = st[0]
        for hi in range(1, n_heads):
            top = jnp.maximum(top, st[2 * hi])
        return (jnp.max(top) > SB_EXIT2).astype(jnp.int32)

    zc, za = jnp.zeros((tq, 1), F32), jnp.zeros((tq, LANES), F32)
    init = (zc, za) * n_heads
    st = lax.cond(i >= 1,
                  lambda: block(i - 1, block(i, init, True), False),
                  lambda: block(i, init, True))

    def cond(c):
        return (c[0] >= 0) & (c[1] > 0)

    def body(c):
        st = block(c[0], c[2:], False)
        return (c[0] - 1, alive(st)) + st

    res = lax.while_loop(cond, body, (i - 2, alive(st)) + st)
    for pr_i in range(ATT_PAIRS):
        o_ref[:, pr_i * LANES:(pr_i + 1) * LANES] = jnp.where(
            lane < HEAD_DIM, res[2 + 4 * pr_i + 1], res[2 + 4 * pr_i + 3]).astype(o_ref.dtype)


def _sb_prompt(zb, kt, vt, layer, batch, seq):
    nq = seq // MOBA_BLOCK
    wq = ATT_PAIRS * LANES
    cq = ZP_QB // wq
    kv_spec = pl.BlockSpec((None, None, wq, seq), lambda b, p, i: (layer, b, p, 0))
    return pl.pallas_call(
        _sb_body, grid=(batch, N_HEADS // 2 // ATT_PAIRS, nq),
        in_specs=[pl.BlockSpec((MOBA_BLOCK, wq), lambda b, p, i: (b * nq + i, cq + p)),
                  kv_spec, kv_spec],
        out_specs=pl.BlockSpec((MOBA_BLOCK, wq), lambda b, p, i: (b * nq + i, p)),
        out_shape=jax.ShapeDtypeStruct((batch * seq, W_ATT), BF16),
        compiler_params=_params(("parallel", "parallel", "arbitrary")),
        name="sb_prompt")(zb, kt, vt)


def _group_norm(v, g, b):
    hp = lax.Precision.HIGHEST
    gi = lax.broadcasted_iota(jnp.int32, (W_C, W_C), 0) // HEAD_DIM
    gj = lax.broadcasted_iota(jnp.int32, (W_C, W_C), 1) // HEAD_DIM
    gmean = jnp.where(gi == gj, 1.0 / HEAD_DIM, 0.0).astype(F32)
    mu = _dot(v, gmean, precision=hp)
    d = v - mu
    var = _dot(d * d, gmean, precision=hp)
    return d * lax.rsqrt(var + LN_EPS) * g + b


def _gmlp_prompt_body(u_ref, v_ref, ws_ref, bias_ref, g_ref, b_ref, o_ref, *, rows):
    ti = lax.broadcasted_iota(jnp.int32, (CHUNK, CHUNK), 0)
    tj = lax.broadcasted_iota(jnp.int32, (CHUNK, CHUNK), 1)
    lane = lax.broadcasted_iota(jnp.int32, (CHUNK, LANES), 1)
    wpair = []
    for pr in range(2):
        w0 = jnp.where(tj <= ti, ws_ref[2 * pr], 0.0)
        w1 = jnp.where(tj <= ti, ws_ref[2 * pr + 1], 0.0)
        wpair.append(jnp.concatenate([w0, w1], axis=1).astype(BF16))
    for r in range(rows // CHUNK):
        sl = slice(r * CHUNK, (r + 1) * CHUNK)
        u = jax.nn.gelu(u_ref[sl, :].astype(F32))
        vn = _group_norm(jax.nn.gelu(v_ref[sl, :].astype(F32)), g_ref[...], b_ref[...])
        mixes = []
        for pr in range(2):
            vp = vn[:, pr * LANES:(pr + 1) * LANES]
            rhs = jnp.concatenate([jnp.where(lane < HEAD_DIM, vp, 0.0),
                                   jnp.where(lane >= HEAD_DIM, vp, 0.0)], axis=0).astype(BF16)
            mixes.append(_dot(wpair[pr], rhs))
        mix = jnp.concatenate(mixes, axis=1) + bias_ref[...]
        o_ref[sl, :] = (u * mix).astype(o_ref.dtype)


def _gmlp_prompt(zb, ws, bias, g, b):
    n = zb.shape[0]
    rows = 512
    return pl.pallas_call(
        functools.partial(_gmlp_prompt_body, rows=rows), grid=(n // rows,),
        in_specs=[pl.BlockSpec((rows, W_C), lambda i: (i, ZP_UC // W_C)),
                  pl.BlockSpec((rows, W_C), lambda i: (i, ZP_VC // W_C)),
                  pl.BlockSpec((C_GROUPS, CHUNK, CHUNK), lambda i: (0, 0, 0)),
                  pl.BlockSpec((CHUNK, W_C), lambda i: (0, 0)),
                  pl.BlockSpec((1, W_C), lambda i: (0, 0)),
                  pl.BlockSpec((1, W_C), lambda i: (0, 0))],
        out_specs=pl.BlockSpec((rows, W_C), lambda i: (i, 0)),
        out_shape=jax.ShapeDtypeStruct((n, W_C), BF16),
        compiler_params=_params(("parallel",)), name="gmlp_prompt")(
            zb, zb, ws, bias, g.reshape(1, W_C), b.reshape(1, W_C))


def _gmlp_sample_body(u_ref, v_ref, coef_ref, bias_ref, g_ref, b_ref, o_ref, vn_ref, *, t_new, bsz):
    vns = []
    for t in range(t_new):
        sl = slice(t * bsz, (t + 1) * bsz)
        vn = _group_norm(jax.nn.gelu(v_ref[sl, :]), g_ref[...], b_ref[...])
        vn_ref[sl, :] = vn
        vns.append(vn)
    for t in range(t_new):
        sl = slice(t * bsz, (t + 1) * bsz)
        mix = bias_ref[t:t + 1, :]
        for s in range(t + 1):
            mix = mix + coef_ref[t * t_new + s:t * t_new + s + 1, :] * vns[s]
        o_ref[sl, :] = (jax.nn.gelu(u_ref[sl, :]) * mix).astype(o_ref.dtype)


def _gmlp_sample(zs, coef, bias, g, b, t_new, bsz):
    n = zs.shape[0]
    full = lambda shape: pl.BlockSpec(shape, lambda i: (0, 0))
    return pl.pallas_call(
        functools.partial(_gmlp_sample_body, t_new=t_new, bsz=bsz), grid=(1,),
        in_specs=[pl.BlockSpec((n, W_C), lambda i: (0, OFF_UC // W_C)),
                  pl.BlockSpec((n, W_C), lambda i: (0, OFF_VC // W_C)),
                  full((t_new * t_new, W_C)), full((t_new, W_C)), full((1, W_C)), full((1, W_C))],
        out_specs=[full((n, W_C)), full((n, W_C))],
        out_shape=[jax.ShapeDtypeStruct((n, W_C), BF16), jax.ShapeDtypeStruct((n, W_C), F32)],
        compiler_params=_params(("arbitrary",)), name="gmlp_sample")(
            zs, zs, coef, bias, g.reshape(1, W_C), b.reshape(1, W_C))


def _out_proj_body(ma_ref, mb_ref, mc_ref, h_ref, w_ref, g_ref, b_ref, o_ref, *, alpha):
    mixed = jnp.concatenate([ma_ref[...].astype(BF16), mb_ref[...].astype(BF16),
                             mc_ref[...].astype(BF16)], axis=1)
    y = _dot(mixed, w_ref[...])
    o_ref[...] = _ln(alpha * h_ref[...] + y, g_ref[...], b_ref[...])


def _out_proj(ma, mb, mc, h, w, g, b, alpha):
    n = h.shape[0]
    tm = min(512, n)
    row = lambda i: (i, 0)
    const = lambda i: (0, 0)
    return pl.pallas_call(
        functools.partial(_out_proj_body, alpha=alpha), grid=(n // tm,),
        in_specs=[pl.BlockSpec((tm, W_ATT), row), pl.BlockSpec((tm, W_ATT), row),
                  pl.BlockSpec((tm, W_C), row), pl.BlockSpec((tm, D_MODEL), row),
                  pl.BlockSpec((D_MODEL, D_MODEL), const),
                  pl.BlockSpec((1, D_MODEL), const), pl.BlockSpec((1, D_MODEL), const)],
        out_specs=pl.BlockSpec((tm, D_MODEL), row),
        out_shape=jax.ShapeDtypeStruct((n, D_MODEL), F32),
        compiler_params=_params(("parallel",)), name="out_proj")(
            ma, mb, mc, h, w, g.reshape(1, D_MODEL), b.reshape(1, D_MODEL))


def _ffn_body(h_ref, w1_ref, w2_ref, g_ref, b_ref, o_ref, acc_ref, *, alpha):
    h = h_ref[...]
    hb = h.astype(BF16)
    step = 512
    for c in range(0, D_FF, step):
        a = _dot(hb, w1_ref[:, c:c + step])
        a = jnp.square(jnp.maximum(a, 0.0)).astype(BF16)
        y = _dot(a, w2_ref[c:c + step, :])
        if c == 0:
            acc_ref[...] = y
        else:
            acc_ref[...] += y
    o_ref[...] = _ln(alpha * h + acc_ref[...], g_ref[...], b_ref[...])


def _ffn(h, w1, w2, g, b, alpha):
    n = h.shape[0]
    tm = min(512, n)
    row = lambda i: (i, 0)
    const = lambda i: (0, 0)
    return pl.pallas_call(
        functools.partial(_ffn_body, alpha=alpha), grid=(n // tm,),
        in_specs=[pl.BlockSpec((tm, D_MODEL), row),
                  pl.BlockSpec((D_MODEL, D_FF), const, pipeline_mode=pl.Buffered(1)),
                  pl.BlockSpec((D_FF, D_MODEL), const, pipeline_mode=pl.Buffered(1)),
                  pl.BlockSpec((1, D_MODEL), const), pl.BlockSpec((1, D_MODEL), const)],
        out_specs=pl.BlockSpec((tm, D_MODEL), row),
        out_shape=jax.ShapeDtypeStruct((n, D_MODEL), F32),
        scratch_shapes=[pltpu.VMEM((tm, D_MODEL), F32)],
        compiler_params=_params(("parallel",)), name="ffn")(
            h, w1, w2, g.reshape(1, D_MODEL), b.reshape(1, D_MODEL))


PAGES_PER_BLK = MOBA_BLOCK // PAGE
B_SLOTS = 3


def _expand_heads(q):
    r = lax.broadcasted_iota(jnp.int32, (8, W_ATT), 0)
    c = lax.broadcasted_iota(jnp.int32, (8, W_ATT), 1) // HEAD_DIM
    pieces = [jnp.where(r == c, jnp.broadcast_to(row, (8, W_ATT)), 0.0) for row in q]
    return jnp.concatenate(pieces, axis=0)


def _reduce_heads(o, out_ref, b):
    r = lax.broadcasted_iota(jnp.int32, (8, W_ATT), 0)
    c = lax.broadcasted_iota(jnp.int32, (8, W_ATT), 1) // HEAD_DIM
    for t in range(o.shape[0] // 8):
        piece = jnp.where(r == c, o[8 * t:8 * t + 8, :], 0.0)
        out_ref[t, pl.ds(b, 1), :] = jnp.sum(piece, axis=0, keepdims=True)


def _sample_attn_body(pt_ref, zs_ref, cka, cva, ckb, cvb, oa_ref, ob_ref,
                      ka_buf, va_buf, kb_buf, vb_buf, sem_a, sem_b,
                      *, layer, bsz, n_pages, t_new, slopes):
    b = pl.program_id(0)
    past_len = n_pages * PAGE
    n_blk = n_pages // PAGES_PER_BLK

    def blk_copies(src_k, src_v, dst_k, dst_v, sem, first_page, slot):
        cps = []
        for k in range(PAGES_PER_BLK):
            pg = pt_ref[first_page + k]
            win = pl.ds(k * PAGE, PAGE)
            cps.append(pltpu.make_async_copy(src_k.at[layer, pg], dst_k.at[slot, :, win],
                                             sem.at[slot, 2 * k]))
            cps.append(pltpu.make_async_copy(src_v.at[layer, pg], dst_v.at[slot, :, win],
                                             sem.at[slot, 2 * k + 1]))
        return cps

    def a_copies(seq):
        cps = []
        for k in range(n_blk):
            cps += blk_copies(cka, cva, ka_buf, va_buf, sem_a,
                              seq * n_pages + k * PAGES_PER_BLK, (seq % 2) * n_blk + k)
        return cps

    def b_copies(seq, back, slot):
        return blk_copies(ckb, cvb, kb_buf, vb_buf, sem_b,
                          seq * n_pages + (n_blk - 1 - back) * PAGES_PER_BLK, slot)

    @pl.when(b == 0)
    def _prime():
        for cp in a_copies(0) + b_copies(0, 0, 0):
            cp.start()

    for cp in a_copies(b) + b_copies(b, 0, b % 2):
        cp.wait()

    @pl.when(b + 1 < bsz)
    def _prefetch_next():
        for cp in a_copies(b + 1) + b_copies(b + 1, 0, (b + 1) % 2):
            cp.start()

    scale = HEAD_DIM ** -0.5
    rows = 8 * t_new
    ri = lax.broadcasted_iota(jnp.int32, (rows, 1), 0)
    tok = ri // 8
    hd = ri % 8
    slope = jnp.zeros((rows, 1), F32)
    for h in range(N_HEADS):
        slope = jnp.where(hd == h, float(slopes[h]), slope)

    def seg(t, off):
        return zs_ref[t, pl.ds(b, 1), off:off + W_ATT]

    qa = _expand_heads([seg(t, OFF_QA) * scale for t in range(t_new)])
    qb = _expand_heads([seg(t, OFF_QB) * scale for t in range(t_new)])
    upper2 = _upper2(MOBA_BLOCK)

    sa, ys = [], []
    for s in range(t_new):
        sc = jnp.sum(qa * seg(s, OFF_KA), axis=1, keepdims=True)
        sc = sc - slope * (tok - s).astype(F32)
        sa.append(jnp.where(tok >= s, sc, -jnp.inf))
        ys.append(jnp.sum(qb * seg(s, OFF_KB), axis=1, keepdims=True) * LOG2E)
    m = sa[0]
    for s in range(1, t_new):
        m = jnp.maximum(m, sa[s])
    l = jnp.zeros((rows, 1), F32)
    acc = jnp.zeros((rows, W_ATT), F32)
    for s in range(t_new):
        pr = jnp.exp(sa[s] - m)
        l = l + pr
        acc = acc + pr * seg(s, OFF_VA)
    parts = [(m, l, acc)]
    carry = jnp.zeros((rows, 1), F32)
    accb = jnp.zeros((rows, W_ATT), F32)
    for s in range(t_new - 1, -1, -1):
        valid = tok > s
        sp = _softplus2(ys[s])
        a = jnp.where(valid, jnp.exp2(ys[s] - sp + carry), 0.0)
        accb = accb + a * seg(s, OFF_VB)
        carry = carry - jnp.where(valid, sp, 0.0)
    qa_b = qa.astype(BF16)
    qb_b = qb.astype(BF16)
    kcol = lax.broadcasted_iota(jnp.int32, (rows, MOBA_BLOCK), 1)

    gates = []
    for blk in range(n_blk):
        slot = (b % 2) * n_blk + blk
        raw = _dot(qa_b, ka_buf[slot].astype(BF16))
        gates.append(jnp.sum(raw, axis=1, keepdims=True))
        dist = (past_len + tok - blk * MOBA_BLOCK - kcol).astype(F32)
        sc = raw - slope * dist
        m = jnp.max(sc, axis=1, keepdims=True)
        pr = jnp.exp(sc - m)
        parts.append((m, jnp.sum(pr, axis=1, keepdims=True),
                      _dot_nt(pr.astype(BF16), va_buf[slot].astype(BF16))))

    def sb_block(slot, carry, accb):
        y = _dot(qb_b, kb_buf[slot].astype(BF16)) * LOG2E
        sp = _softplus2(y)
        lk = -sp
        aft = _suffix_sums(lk, upper2)
        a = jnp.exp2(y - sp + aft + carry)
        accb = accb + _dot_nt(a.astype(BF16), vb_buf[slot].astype(BF16))
        return carry + aft[:, 0:1] + lk[:, 0:1], accb

    def alive(carry):
        return (jnp.max(carry) > SB_EXIT2).astype(jnp.int32)

    carry, accb = sb_block(b % 2, carry, accb)

    def older_cond(c):
        return (c[0] < n_blk) & (c[1] > 0)

    def older_body(c):
        cps = b_copies(b, c[0], B_SLOTS - 1)
        for cp in cps:
            cp.start()
        for cp in cps:
            cp.wait()
        carry, accb = sb_block(B_SLOTS - 1, c[2], c[3])
        return c[0] + 1, alive(carry), carry, accb

    accb = lax.while_loop(older_cond, older_body, (jnp.int32(1), alive(carry), carry, accb))[3]

    sel = []
    for nb in range(n_blk):
        rank = jnp.zeros((rows, 1), jnp.int32)
        for mth in range(n_blk):
            if mth < nb:
                rank += jnp.where(gates[mth] >= gates[nb], 1, 0)
            elif mth > nb:
                rank += jnp.where(gates[mth] > gates[nb], 1, 0)
        sel.append(rank < MOBA_TOPK)
    big = parts[0][0]
    for nb in range(n_blk):
        big = jnp.maximum(big, jnp.where(sel[nb], parts[nb + 1][0], -jnp.inf))
    w = jnp.exp(parts[0][0] - big)
    l = w * parts[0][1]
    acc = w * parts[0][2]
    for nb in range(n_blk):
        m_nb, l_nb, acc_nb = parts[nb + 1]
        w = jnp.where(sel[nb], jnp.exp(jnp.minimum(m_nb - big, 0.0)), 0.0)
        l = l + w * l_nb
        acc = acc + w * acc_nb
    _reduce_heads(acc / l, oa_ref, b)
    _reduce_heads(accb, ob_ref, b)


def _sample_attn(zs3, caches, layer, page_table, slopes):
    t_new, bsz, _ = zs3.shape
    n_pages = page_table.shape[1]
    n_blk = n_pages // PAGES_PER_BLK
    assert n_pages % PAGES_PER_BLK == 0
    hbm = pl.BlockSpec(memory_space=pl.ANY)
    blk_buf = lambda n: pltpu.VMEM((n, W_ATT, MOBA_BLOCK), F32)
    out_spec = pl.BlockSpec((t_new, bsz, W_ATT), lambda b, pt: (0, 0, 0))
    grid_spec = pltpu.PrefetchScalarGridSpec(
        num_scalar_prefetch=1, grid=(bsz,),
        in_specs=[pl.BlockSpec((t_new, bsz, IN_WIDTH), lambda b, pt: (0, 0, 0)),
                  hbm, hbm, hbm, hbm],
        out_specs=[out_spec, out_spec],
        scratch_shapes=[blk_buf(2 * n_blk), blk_buf(2 * n_blk),
                        blk_buf(B_SLOTS), blk_buf(B_SLOTS),
                        pltpu.SemaphoreType.DMA((2 * n_blk, 2 * PAGES_PER_BLK)),
                        pltpu.SemaphoreType.DMA((B_SLOTS, 2 * PAGES_PER_BLK))])
    body = functools.partial(_sample_attn_body, layer=layer, bsz=bsz, n_pages=n_pages,
                             t_new=t_new, slopes=tuple(float(s) for s in slopes))
    return pl.pallas_call(
        body, grid_spec=grid_spec,
        out_shape=[jax.ShapeDtypeStruct((t_new, bsz, W_ATT), F32)] * 2,
        compiler_params=_params(("arbitrary",)), name="sample_attn")(
            page_table.reshape(-1), zs3, *caches)


def _gate_matrix(kmean_t):
    rows = jnp.arange(W_ATT)[:, None] // HEAD_DIM
    cols = jnp.arange(LANES)[None, :] // 16
    tiled = jnp.tile(kmean_t[:, :, :16], (1, 1, LANES // 16))
    return jnp.where(rows == cols, tiled, 0.0)


def kernel(x_prompt, x_sample, cache_k_a, cache_v_a, cache_k_b, cache_v_b, page_table, ln_in_g, ln_in_b, w_in, w_out, w_spatial, b_spatial, ln_c_g, ln_c_b, ln1_g, ln1_b, w_ff1, w_ff2, ln2_g, ln2_b):
    batch, seq, _ = x_prompt.shape
    dec_batch, dec_seq, _ = x_sample.shape
    depth = w_in.shape[0]
    alpha = (2 * depth) ** 0.25
    slopes_np = _alibi_slopes(N_HEADS)
    slopes = jnp.asarray(slopes_np)
    n_blk = seq // MOBA_BLOCK
    assert n_blk <= 16 and seq % 512 == 0 and dec_seq <= CHUNK and dec_batch % 8 == 0
    assert all(math.frexp(float(s))[0] == 0.5 for s in slopes_np)

    w_in_b = w_in.astype(BF16)
    w_q = jnp.concatenate([w_in_b[:, :, OFF_QA:OFF_QA + W_ATT], w_in_b[:, :, OFF_QB:OFF_QB + W_ATT],
                           w_in_b[:, :, OFF_UC:]], axis=2)
    w_kv_t = jnp.swapaxes(jnp.concatenate(
        [w_in_b[:, :, OFF_KA:OFF_KA + 2 * W_ATT], w_in_b[:, :, OFF_KB:OFF_KB + 2 * W_ATT]],
        axis=2), 1, 2)
    w_out_b = w_out.astype(BF16)
    w_ff1_b = w_ff1.astype(BF16)
    w_ff2_b = w_ff2.astype(BF16)
    caches = [jnp.transpose(c, (0, 1, 3, 4, 2)).reshape(c.shape[0], c.shape[1], W_ATT, PAGE)
              for c in (cache_k_a, cache_v_a, cache_k_b, cache_v_b)]

    bias_p = jnp.repeat(jnp.swapaxes(b_spatial, 1, 2), HEAD_DIM, axis=2)
    coef_s = jnp.repeat(jnp.transpose(w_spatial[:, :, :dec_seq, :dec_seq], (0, 2, 3, 1)),
                        HEAD_DIM, axis=3).reshape(depth, dec_seq * dec_seq, W_C)
    bias_s = jnp.repeat(jnp.swapaxes(b_spatial[:, :, :dec_seq], 1, 2), HEAD_DIM, axis=2)

    hp = x_prompt.reshape(batch * seq, D_MODEL)
    hs = jnp.swapaxes(x_sample, 0, 1).reshape(dec_seq * dec_batch, D_MODEL)
    kv_p = [jnp.zeros((depth, batch, W_ATT, seq), F32) for _ in range(4)]
    kv_s = [jnp.zeros((depth, dec_seq, W_ATT, dec_batch), F32) for _ in range(4)]
    vc_s = []
    for l in range(depth):
        ln = (ln_in_g, ln_in_b) if l == 0 else None
        lnc_g, lnc_b = ln_c_g[l].reshape(-1), ln_c_b[l].reshape(-1)

        hp, zb, kv_p, kmean_t = _in_proj_prompt(hp, w_q[l], w_kv_t[l], ln, kv_p, l, depth, batch, seq)
        selb = _moba_gate(zb, _gate_matrix(kmean_t), batch, seq)
        mix_a = _moba_prompt(zb, kv_p[0], kv_p[1], selb, slopes, l, batch, seq)
        mix_b = _sb_prompt(zb, kv_p[2], kv_p[3], l, batch, seq)
        mix_c = _gmlp_prompt(zb, w_spatial[l], bias_p[l], lnc_g, lnc_b)
        h1 = _out_proj(mix_a, mix_b, mix_c, hp, w_out_b[l], ln1_g[l], ln1_b[l], alpha)
        hp = _ffn(h1, w_ff1_b[l], w_ff2_b[l], ln2_g[l], ln2_b[l], alpha)

        hs, zs, kv_s = _in_proj_sample(hs, w_in_b[l], w_kv_t[l], ln, kv_s, l, depth, dec_seq, dec_batch)
        oa, ob = _sample_attn(zs.reshape(dec_seq, dec_batch, IN_WIDTH), caches, l,
                              page_table, slopes_np)
        mix_c, vn = _gmlp_sample(zs, coef_s[l], bias_s[l], lnc_g, lnc_b, dec_seq, dec_batch)
        vc_s.append(vn)
        h1 = _out_proj(oa.reshape(-1, W_ATT), ob.reshape(-1, W_ATT), mix_c, hs,
                       w_out_b[l], ln1_g[l], ln1_b[l], alpha)
        hs = _ffn(h1, w_ff1_b[l], w_ff2_b[l], ln2_g[l], ln2_b[l], alpha)

    kv_out_p = [jnp.transpose(a.reshape(depth, batch, N_HEADS, HEAD_DIM, seq), (0, 1, 4, 2, 3))
                for a in kv_p]
    kv_out_s = [jnp.transpose(a.reshape(depth, dec_seq, N_HEADS, HEAD_DIM, dec_batch), (0, 4, 1, 2, 3))
                for a in kv_s]
    vc_out = jnp.swapaxes(jnp.stack(vc_s).reshape(depth, dec_seq, dec_batch, W_C), 1, 2)
    y_s = jnp.swapaxes(hs.reshape(dec_seq, dec_batch, D_MODEL), 0, 1)
    return (hp.reshape(batch, seq, D_MODEL), y_s, *kv_out_p, *kv_out_s, vc_out)
```

```python
import functools
import math

import jax
import jax.numpy as jnp
import numpy as np
from jax import lax
from jax.experimental import pallas as pl
from jax.experimental.pallas import tpu as pltpu

F32 = jnp.float32
BF16 = jnp.bfloat16

D_MODEL = 1024
HEAD_DIM = 64
N_HEADS = 6
W_ATT = N_HEADS * HEAD_DIM
C_GROUPS = 4
W_C = C_GROUPS * HEAD_DIM
IN_WIDTH = 6 * W_ATT + 2 * W_C
D_FF = 4 * D_MODEL
MOBA_BLOCK = 256
MOBA_TOPK = 3
CHUNK = 128
PAGE = 128
LN_EPS = 1e-5
LANES = 128
NEG_BIG = -1e30
SB_EXIT = -110.0

OFF_QA, OFF_KA, OFF_VA = 0, W_ATT, 2 * W_ATT
OFF_QB, OFF_KB, OFF_VB = 3 * W_ATT, 4 * W_ATT, 5 * W_ATT
OFF_UC, OFF_VC = 6 * W_ATT, 6 * W_ATT + W_C
ZP_QA, ZP_QB, ZP_UC, ZP_VC = 0, W_ATT, 2 * W_ATT, 2 * W_ATT + W_C
ZP_WIDTH = 2 * W_ATT + 2 * W_C

VMEM_LIMIT = 56 * 1024 * 1024


def _alibi_slopes(n):
    def pow2(m):
        start = 2.0 ** (-8.0 / m)
        return [start ** (i + 1) for i in range(m)]
    p = 2 ** int(math.floor(math.log2(n)))
    s = pow2(p)
    if p < n:
        s = s + pow2(2 * p)[0::2][: n - p]
    return np.array(s, dtype=np.float32)


def _ln(x, g, b):
    mu = jnp.mean(x, axis=-1, keepdims=True)
    d = x - mu
    var = jnp.mean(d * d, axis=-1, keepdims=True)
    return d * lax.rsqrt(var + LN_EPS) * g + b


def _dot(a, b, precision=None):
    return jnp.dot(a, b, preferred_element_type=F32, precision=precision)


def _dot_nt(a, b, precision=None):
    return lax.dot_general(a, b, (((1,), (1,)), ((), ())),
                           preferred_element_type=F32, precision=precision)


def _params(sem):
    return pltpu.CompilerParams(dimension_semantics=sem, vmem_limit_bytes=VMEM_LIMIT)


def _in_proj_prompt_body(*refs, apply_ln, tm):
    it = iter(refs)
    x_ref = next(it)
    if apply_ln:
        g_ref, b_ref = next(it), next(it)
    wq_ref, wkv_ref = next(it), next(it)
    for _ in range(4):
        next(it)
    if apply_ln:
        h_ref = next(it)
    zb_ref = next(it)
    kv_refs = [next(it) for _ in range(4)]
    km_ref = next(it)
    ti = pl.program_id(1)

    x = x_ref[...]
    if apply_ln:
        x = _ln(x, g_ref[...], b_ref[...])
        h_ref[...] = x
    hb = x.astype(BF16)
    for c in range(0, ZP_WIDTH, 256):
        zb_ref[:, c:c + 256] = _dot(hb, wq_ref[:, c:c + 256]).astype(BF16)
    lane = lax.broadcasted_iota(jnp.int32, (W_ATT, LANES), 1)

    @pl.when(ti == 0)
    def _zero_means():
        km_ref[...] = jnp.zeros_like(km_ref)

    kt_all = _dot_nt(wkv_ref[...], hb)
    for t in range(4):
        kt = kt_all[t * W_ATT:(t + 1) * W_ATT]
        kv_refs[t][...] = kt
        if t == 0:
            km = km_ref[...]
            for r in range(tm // MOBA_BLOCK):
                mean = jnp.sum(kt[:, r * MOBA_BLOCK:(r + 1) * MOBA_BLOCK], axis=1,
                               keepdims=True) * (1.0 / MOBA_BLOCK)
                km = jnp.where(lane == ti * (tm // MOBA_BLOCK) + r, mean, km)
            km_ref[...] = km


def _in_proj_prompt(x, wq, wkv, ln, kv_prev, layer, depth, batch, seq):
    n = x.shape[0]
    tm = 512
    nt = seq // tm
    row = lambda b, t: (b * nt + t, 0)
    const = lambda b, t: (0, 0)
    args, in_specs = [x], [pl.BlockSpec((tm, D_MODEL), row)]
    if ln is not None:
        args += [ln[0].reshape(1, D_MODEL), ln[1].reshape(1, D_MODEL)]
        in_specs += [pl.BlockSpec((1, D_MODEL), const)] * 2
    args += [wq, wkv]
    in_specs += [pl.BlockSpec((D_MODEL, ZP_WIDTH), const), pl.BlockSpec((4 * W_ATT, D_MODEL), const)]
    out_shape, out_specs = [], []
    if ln is not None:
        out_shape.append(jax.ShapeDtypeStruct((n, D_MODEL), F32))
        out_specs.append(pl.BlockSpec((tm, D_MODEL), row))
    out_shape.append(jax.ShapeDtypeStruct((n, ZP_WIDTH), BF16))
    out_specs.append(pl.BlockSpec((tm, ZP_WIDTH), row))
    aliases = {}
    for t in range(4):
        aliases[len(args)] = len(out_shape)
        args.append(kv_prev[t])
        in_specs.append(pl.BlockSpec(memory_space=pl.ANY))
        out_shape.append(jax.ShapeDtypeStruct((depth, batch, W_ATT, seq), F32))
        out_specs.append(pl.BlockSpec((None, None, W_ATT, tm), lambda b, t: (layer, b, 0, t)))
    out_shape.append(jax.ShapeDtypeStruct((batch, W_ATT, LANES), F32))
    out_specs.append(pl.BlockSpec((None, W_ATT, LANES), lambda b, t: (b, 0, 0)))
    body = functools.partial(_in_proj_prompt_body, apply_ln=ln is not None, tm=tm)
    outs = list(pl.pallas_call(
        body, grid=(batch, nt), in_specs=in_specs, out_specs=out_specs, out_shape=out_shape,
        input_output_aliases=aliases, compiler_params=_params(("parallel", "arbitrary")),
        name="in_proj_prompt")(*args))
    h = outs.pop(0) if ln is not None else x
    return h, outs[0], outs[1:5], outs[5]


def _in_proj_sample_body(*refs, apply_ln, t_new, bsz):
    it = iter(refs)
    x_ref = next(it)
    if apply_ln:
        g_ref, b_ref = next(it), next(it)
    w_ref, wkv_ref = next(it), next(it)
    for _ in range(4):
        next(it)
    if apply_ln:
        h_ref = next(it)
    zs_ref = next(it)
    kv_refs = [next(it) for _ in range(4)]

    x = x_ref[...]
    if apply_ln:
        x = _ln(x, g_ref[...], b_ref[...])
        h_ref[...] = x
    hb = x.astype(BF16)
    for c in range(0, IN_WIDTH, 256):
        zs_ref[:, c:c + 256] = _dot(hb, w_ref[:, c:c + 256])
    for t in range(4):
        kt = _dot_nt(wkv_ref[t * W_ATT:(t + 1) * W_ATT, :], hb)
        for tt in range(t_new):
            kv_refs[t][tt] = kt[:, tt * bsz:(tt + 1) * bsz]


def _in_proj_sample(x, w, wkv, ln, kv_prev, layer, depth, t_new, bsz):
    n = x.shape[0]
    const = lambda i: (0, 0)
    args, in_specs = [x], [pl.BlockSpec((n, D_MODEL), const)]
    if ln is not None:
        args += [ln[0].reshape(1, D_MODEL), ln[1].reshape(1, D_MODEL)]
        in_specs += [pl.BlockSpec((1, D_MODEL), const)] * 2
    args += [w, wkv]
    in_specs += [pl.BlockSpec((D_MODEL, IN_WIDTH), const), pl.BlockSpec((4 * W_ATT, D_MODEL), const)]
    out_shape, out_specs = [], []
    if ln is not None:
        out_shape.append(jax.ShapeDtypeStruct((n, D_MODEL), F32))
        out_specs.append(pl.BlockSpec((n, D_MODEL), const))
    out_shape.append(jax.ShapeDtypeStruct((n, IN_WIDTH), F32))
    out_specs.append(pl.BlockSpec((n, IN_WIDTH), const))
    aliases = {}
    for t in range(4):
        aliases[len(args)] = len(out_shape)
        args.append(kv_prev[t])
        in_specs.append(pl.BlockSpec(memory_space=pl.ANY))
        out_shape.append(jax.ShapeDtypeStruct((depth, t_new, W_ATT, bsz), F32))
        out_specs.append(pl.BlockSpec((None, t_new, W_ATT, bsz), lambda i: (layer, 0, 0, 0)))
    body = functools.partial(_in_proj_sample_body, apply_ln=ln is not None, t_new=t_new, bsz=bsz)
    outs = list(pl.pallas_call(
        body, grid=(1,), in_specs=in_specs, out_specs=out_specs, out_shape=out_shape,
        input_output_aliases=aliases, compiler_params=_params(("arbitrary",)),
        name="in_proj_sample")(*args))
    h = outs.pop(0) if ln is not None else x
    return h, outs[0], outs[1:5]


def _gate_body(q_ref, km_ref, o_ref):
    cur = pl.program_id(1)
    g = _dot(q_ref[...].astype(F32), km_ref[...], precision=lax.Precision.HIGHEST)
    blk = lax.broadcasted_iota(jnp.int32, g.shape, 1) & 15
    past = blk < cur
    g = jnp.where(past, g, -jnp.inf)
    rank = jnp.zeros(g.shape, jnp.int32)
    for s in range(1, 16):
        lower = pltpu.roll(g, s, axis=1)
        rank += jnp.where((blk >= s) & (lower >= g), 1, 0)
        upper = pltpu.roll(g, LANES - s, axis=1)
        rank += jnp.where((blk + s <= 15) & (upper > g), 1, 0)
    o_ref[...] = jnp.where(past & (rank < MOBA_TOPK), 0.0, NEG_BIG)


def _moba_gate(zb, km, batch, seq):
    nq = seq // MOBA_BLOCK
    return pl.pallas_call(
        _gate_body, grid=(batch, nq),
        in_specs=[pl.BlockSpec((MOBA_BLOCK, W_ATT), lambda b, i: (b * nq + i, ZP_QA // W_ATT)),
                  pl.BlockSpec((None, W_ATT, LANES), lambda b, i: (b, 0, 0))],
        out_specs=pl.BlockSpec((MOBA_BLOCK, LANES), lambda b, i: (b * nq + i, 0)),
        out_shape=jax.ShapeDtypeStruct((batch * seq, LANES), F32),
        compiler_params=_params(("parallel", "parallel")), name="moba_gate")(zb, km)


ROLE_SEL0, ROLE_SEL1 = 0, 1
ROLE_ROW, ROLE_KEY = 96, 97
ROLE_BLK, ROLE_2ND = 98, 99


ATT_PAIRS = 3


def _moba_body(slopes_ref, q_ref, kt_ref, vt_ref, selb_ref, o_ref, lhs_s, rhs_s):
    pg = pl.program_id(1)
    i = pl.program_id(2)
    tq = MOBA_BLOCK
    selb = selb_ref[...]
    lane = lax.broadcasted_iota(jnp.int32, (2 * tq, LANES), 1)
    row2 = lax.broadcasted_iota(jnp.int32, (2 * tq, LANES), 0)
    head1 = row2 >= tq
    rowf = jnp.where(head1, row2 - tq, row2).astype(F32)
    role = lax.broadcasted_iota(jnp.int32, (LANES, 2 * tq), 0)
    key2 = lax.broadcasted_iota(jnp.int32, (LANES, 2 * tq), 1)
    second = key2 >= tq
    keyf = jnp.where(second, key2 - tq, key2).astype(F32)
    rhs_roles = jnp.where(
        (role == ROLE_SEL0) & jnp.logical_not(second), 1.0,
        jnp.where(((role == ROLE_SEL1) | (role == ROLE_2ND)) & second, 1.0,
                  jnp.where((role == ROLE_ROW) | (role == ROLE_BLK), 1.0,
                            jnp.where(role == ROLE_KEY, keyf, 0.0)))).astype(BF16)
    rq = jnp.where(lax.broadcasted_iota(jnp.int32, (2 * tq, tq), 0) >= tq,
                   lax.broadcasted_iota(jnp.int32, (2 * tq, tq), 0) - tq,
                   lax.broadcasted_iota(jnp.int32, (2 * tq, tq), 0))
    ck = lax.broadcasted_iota(jnp.int32, (2 * tq, tq), 1)
    d0 = (rq - ck).astype(F32)
    causal = ck <= rq
    row0 = pl.multiple_of(i * tq, tq)
    fixed = (lane == ROLE_ROW) | (lane == ROLE_KEY) | (lane == ROLE_2ND)

    pairs, state = [], []
    for pr_i in range(ATT_PAIRS):
        lanes = slice(pr_i * LANES, (pr_i + 1) * LANES)
        rhs_s[pr_i, LANES:2 * LANES, :] = rhs_roles
        q = q_ref[:, lanes].astype(F32) * (HEAD_DIM ** -0.5)
        q2 = jnp.concatenate([q, q], axis=0)
        qh = jnp.where((lane >= HEAD_DIM) == head1, q2, 0.0).astype(BF16)
        lhs_s[pr_i, :, 0:LANES] = qh
        hd0 = 2 * (pg * ATT_PAIRS + pr_i)
        slope = jnp.where(lax.broadcasted_iota(jnp.int32, (2 * tq, 1), 0) >= tq,
                          slopes_ref[hd0 + 1], slopes_ref[hd0])
        lhs_fix = jnp.where(lane == ROLE_ROW, -slope * rowf,
                            jnp.where(lane == ROLE_KEY, slope, slope * tq))
        pairs.append((hd0, slope, lhs_fix))
        kd = kt_ref[lanes, pl.ds(row0, tq)].astype(BF16)
        vd = vt_ref[lanes, pl.ds(row0, tq)].astype(BF16)
        s = jnp.where(causal, _dot(qh, kd) - slope * d0, -jnp.inf)
        m = jnp.max(s, axis=1, keepdims=True)
        pr = jnp.exp(s - m)
        state += [m, jnp.sum(pr, axis=1, keepdims=True), _dot_nt(pr.astype(BF16), vd)]

    def body(pp, carry):
        j0 = 2 * pp
        r0 = pl.multiple_of(j0 * tq, 2 * tq)
        out = []
        for pr_i in range(ATT_PAIRS):
            lanes = slice(pr_i * LANES, (pr_i + 1) * LANES)
            hd0, slope, lhs_fix = pairs[pr_i]
            m, l, acc = carry[3 * pr_i:3 * pr_i + 3]
            rhs_s[pr_i, 0:LANES, :] = kt_ref[lanes, pl.ds(r0, 2 * tq)].astype(BF16)
            v2 = vt_ref[lanes, pl.ds(r0, 2 * tq)].astype(BF16)
            sel = jnp.concatenate(
                [pltpu.roll(selb, (LANES - (hd0 + hh) * 16 - j0) % LANES, axis=1) for hh in range(2)],
                axis=0)
            off = -slope * ((i - j0) * tq).astype(F32)
            ext = jnp.where(fixed, lhs_fix, jnp.where(lane == ROLE_BLK, off, sel))
            lhs_s[pr_i, :, LANES:2 * LANES] = ext.astype(BF16)
            t = _dot(lhs_s[pr_i], rhs_s[pr_i])
            m_new = jnp.maximum(m, jnp.max(t, axis=1, keepdims=True))
            alpha = jnp.exp(m - m_new)
            pr = jnp.exp(t - m_new)
            l = alpha * l + jnp.sum(pr, axis=1, keepdims=True)
            acc = alpha * acc + _dot_nt(pr.astype(BF16), v2)
            out += [m_new, l, acc]
        return tuple(out)

    state = lax.fori_loop(0, (i + 1) // 2, body, tuple(state))
    for pr_i in range(ATT_PAIRS):
        o = state[3 * pr_i + 2] / state[3 * pr_i + 1]
        o_ref[:, pr_i * LANES:(pr_i + 1) * LANES] = jnp.where(
            lax.broadcasted_iota(jnp.int32, (tq, LANES), 1) < HEAD_DIM, o[:tq], o[tq:]).astype(o_ref.dtype)


def _moba_prompt(zb, kt, vt, selb, slopes, layer, batch, seq):
    nq = seq // MOBA_BLOCK
    wq = ATT_PAIRS * LANES
    cq = ZP_QA // wq
    kv_spec = pl.BlockSpec((None, None, wq, seq), lambda b, p, i, s: (layer, b, p, 0))
    grid_spec = pltpu.PrefetchScalarGridSpec(
        num_scalar_prefetch=1, grid=(batch, N_HEADS // 2 // ATT_PAIRS, nq),
        in_specs=[pl.BlockSpec((MOBA_BLOCK, wq), lambda b, p, i, s: (b * nq + i, cq + p)),
                  kv_spec, kv_spec,
                  pl.BlockSpec((MOBA_BLOCK, LANES), lambda b, p, i, s: (b * nq + i, 0))],
        out_specs=pl.BlockSpec((MOBA_BLOCK, wq), lambda b, p, i, s: (b * nq + i, p)),
        scratch_shapes=[pltpu.VMEM((ATT_PAIRS, 2 * MOBA_BLOCK, 2 * LANES), BF16),
                        pltpu.VMEM((ATT_PAIRS, 2 * LANES, 2 * MOBA_BLOCK), BF16)])
    return pl.pallas_call(
        _moba_body, grid_spec=grid_spec,
        out_shape=jax.ShapeDtypeStruct((batch * seq, W_ATT), BF16),
        compiler_params=_params(("parallel", "parallel", "arbitrary")),
        name="moba_prompt")(slopes, zb, kt, vt, selb)


LOG2E = 1.4426950408889634
SB_EXIT2 = SB_EXIT * LOG2E


def _softplus2(y):
    return jnp.maximum(y, 0.0) + jnp.log2(1.0 + jnp.exp2(-jnp.abs(y)))


def _suffix_sums(lk, upper2):
    hi = lk.astype(BF16)
    lo = (lk - hi.astype(F32)).astype(BF16)
    return _dot(jnp.concatenate([hi, lo], axis=1), upper2)


def _upper2(n):
    r = lax.broadcasted_iota(jnp.int32, (2 * n, n), 0)
    c = lax.broadcasted_iota(jnp.int32, (2 * n, n), 1)
    return jnp.where(jnp.where(r >= n, r - n, r) > c, 1.0, 0.0).astype(BF16)


def _sb_body(q_ref, kt_ref, vt_ref, o_ref):
    i = pl.program_id(2)
    tq = MOBA_BLOCK
    lane = lax.broadcasted_iota(jnp.int32, (2 * tq, LANES), 1)
    row2 = lax.broadcasted_iota(jnp.int32, (2 * tq, tq), 0)
    rq = jnp.where(row2 >= tq, row2 - tq, row2)
    ck = lax.broadcasted_iota(jnp.int32, (2 * tq, tq), 1)
    before = ck < rq
    upper2 = _upper2(tq)
    head1 = lax.broadcasted_iota(jnp.int32, (2 * tq, LANES), 0) >= tq
    qhs = []
    for pr_i in range(ATT_PAIRS):
        q = q_ref[:, pr_i * LANES:(pr_i + 1) * LANES].astype(F32) * (HEAD_DIM ** -0.5)
        q2 = jnp.concatenate([q, q], axis=0)
        qhs.append(jnp.where((lane >= HEAD_DIM) == head1, q2, 0.0).astype(BF16))

    def block(j, st, diag):
        r0 = pl.multiple_of(j * tq, tq)
        ys, sps, lks, vbs = [], [], [], []
        for pr_i in range(ATT_PAIRS):
            lanes = slice(pr_i * LANES, (pr_i + 1) * LANES)
            kb = kt_ref[lanes, pl.ds(r0, tq)].astype(BF16)
            vbs.append(vt_ref[lanes, pl.ds(r0, tq)].astype(BF16))
            y = _dot(qhs[pr_i], kb) * LOG2E
            sp = _softplus2(y)
            lk = -sp
            if diag:
                lk = jnp.where(before, lk, 0.0)
            ys.append(y)
            sps.append(sp)
            lks.append(lk)
        aft_all = _suffix_sums(jnp.concatenate(lks, axis=0), upper2)
        out = []
        for pr_i in range(ATT_PAIRS):
            carry, acc = st[2 * pr_i], st[2 * pr_i + 1]
            aft = aft_all[pr_i * 2 * tq:(pr_i + 1) * 2 * tq]
            a = jnp.exp2(ys[pr_i] - sps[pr_i] + aft + carry)
            if diag:
                a = jnp.where(before, a, 0.0)
            out += [carry + aft[:, 0:1] + lks[pr_i][:, 0:1],
                    acc + _dot_nt(a.astype(BF16), vbs[pr_i])]
        return tuple(out)

    def alive(st):
        top = st[0]
        for pr_i in range(1, ATT_PAIRS):
            top = jnp.maximum(top, st[2 * pr_i])
        return (jnp.max(top) > SB_EXIT2).astype(jnp.int32)

    zc, za = jnp.zeros((2 * tq, 1), F32), jnp.zeros((2 * tq, LANES), F32)
    init = (zc, za) * ATT_PAIRS
    st = lax.cond(i >= 1,
                  lambda: block(i - 1, block(i, init, True), False),
                  lambda: block(i, init, True))

    def cond(c):
        return (c[0] >= 0) & (c[1] > 0)

    def body(c):
        st = block(c[0], c[2:], False)
        return (c[0] - 1, alive(st)) + st

    res = lax.while_loop(cond, body, (i - 2, alive(st)) + st)
    for pr_i in range(ATT_PAIRS):
        acc = res[2 + 2 * pr_i + 1]
        o_ref[:, pr_i * LANES:(pr_i + 1) * LANES] = jnp.where(
            lax.broadcasted_iota(jnp.int32, (tq, LANES), 1) < HEAD_DIM, acc[:tq], acc[tq:]).astype(o_ref.dtype)


def _sb_prompt(zb, kt, vt, layer, batch, seq):
    nq = seq // MOBA_BLOCK
    wq = ATT_PAIRS * LANES
    cq = ZP_QB // wq
    kv_spec = pl.BlockSpec((None, None, wq, seq), lambda b, p, i: (layer, b, p, 0))
    return pl.pallas_call(
        _sb_body, grid=(batch, N_HEADS // 2 // ATT_PAIRS, nq),
        in_specs=[pl.BlockSpec((MOBA_BLOCK, wq), lambda b, p, i: (b * nq + i, cq + p)),
                  kv_spec, kv_spec],
        out_specs=pl.BlockSpec((MOBA_BLOCK, wq), lambda b, p, i: (b * nq + i, p)),
        out_shape=jax.ShapeDtypeStruct((batch * seq, W_ATT), BF16),
        compiler_params=_params(("parallel", "parallel", "arbitrary")),
        name="sb_prompt")(zb, kt, vt)


def _group_norm(v, g, b):
    hp = lax.Precision.HIGHEST
    gi = lax.broadcasted_iota(jnp.int32, (W_C, W_C), 0) // HEAD_DIM
    gj = lax.broadcasted_iota(jnp.int32, (W_C, W_C), 1) // HEAD_DIM
    gmean = jnp.where(gi == gj, 1.0 / HEAD_DIM, 0.0).astype(F32)
    mu = _dot(v, gmean, precision=hp)
    d = v - mu
    var = _dot(d * d, gmean, precision=hp)
    return d * lax.rsqrt(var + LN_EPS) * g + b


def _gmlp_prompt_body(u_ref, v_ref, ws_ref, bias_ref, g_ref, b_ref, o_ref, *, rows):
    ti = lax.broadcasted_iota(jnp.int32, (CHUNK, CHUNK), 0)
    tj = lax.broadcasted_iota(jnp.int32, (CHUNK, CHUNK), 1)
    lane = lax.broadcasted_iota(jnp.int32, (CHUNK, LANES), 1)
    wpair = []
    for pr in range(2):
        w0 = jnp.where(tj <= ti, ws_ref[2 * pr], 0.0)
        w1 = jnp.where(tj <= ti, ws_ref[2 * pr + 1], 0.0)
        wpair.append(jnp.concatenate([w0, w1], axis=1).astype(BF16))
    for r in range(rows // CHUNK):
        sl = slice(r * CHUNK, (r + 1) * CHUNK)
        u = jax.nn.gelu(u_ref[sl, :].astype(F32))
        vn = _group_norm(jax.nn.gelu(v_ref[sl, :].astype(F32)), g_ref[...], b_ref[...])
        mixes = []
        for pr in range(2):
            vp = vn[:, pr * LANES:(pr + 1) * LANES]
            rhs = jnp.concatenate([jnp.where(lane < HEAD_DIM, vp, 0.0),
                                   jnp.where(lane >= HEAD_DIM, vp, 0.0)], axis=0).astype(BF16)
            mixes.append(_dot(wpair[pr], rhs))
        mix = jnp.concatenate(mixes, axis=1) + bias_ref[...]
        o_ref[sl, :] = (u * mix).astype(o_ref.dtype)


def _gmlp_prompt(zb, ws, bias, g, b):
    n = zb.shape[0]
    rows = 512
    return pl.pallas_call(
        functools.partial(_gmlp_prompt_body, rows=rows), grid=(n // rows,),
        in_specs=[pl.BlockSpec((rows, W_C), lambda i: (i, ZP_UC // W_C)),
                  pl.BlockSpec((rows, W_C), lambda i: (i, ZP_VC // W_C)),
                  pl.BlockSpec((C_GROUPS, CHUNK, CHUNK), lambda i: (0, 0, 0)),
                  pl.BlockSpec((CHUNK, W_C), lambda i: (0, 0)),
                  pl.BlockSpec((1, W_C), lambda i: (0, 0)),
                  pl.BlockSpec((1, W_C), lambda i: (0, 0))],
        out_specs=pl.BlockSpec((rows, W_C), lambda i: (i, 0)),
        out_shape=jax.ShapeDtypeStruct((n, W_C), BF16),
        compiler_params=_params(("parallel",)), name="gmlp_prompt")(
            zb, zb, ws, bias, g.reshape(1, W_C), b.reshape(1, W_C))


def _gmlp_sample_body(u_ref, v_ref, coef_ref, bias_ref, g_ref, b_ref, o_ref, vn_ref, *, t_new, bsz):
    vns = []
    for t in range(t_new):
        sl = slice(t * bsz, (t + 1) * bsz)
        vn = _group_norm(jax.nn.gelu(v_ref[sl, :]), g_ref[...], b_ref[...])
        vn_ref[sl, :] = vn
        vns.append(vn)
    for t in range(t_new):
        sl = slice(t * bsz, (t + 1) * bsz)
        mix = bias_ref[t:t + 1, :]
        for s in range(t + 1):
            mix = mix + coef_ref[t * t_new + s:t * t_new + s + 1, :] * vns[s]
        o_ref[sl, :] = (jax.nn.gelu(u_ref[sl, :]) * mix).astype(o_ref.dtype)


def _gmlp_sample(zs, coef, bias, g, b, t_new, bsz):
    n = zs.shape[0]
    full = lambda shape: pl.BlockSpec(shape, lambda i: (0, 0))
    return pl.pallas_call(
        functools.partial(_gmlp_sample_body, t_new=t_new, bsz=bsz), grid=(1,),
        in_specs=[pl.BlockSpec((n, W_C), lambda i: (0, OFF_UC // W_C)),
                  pl.BlockSpec((n, W_C), lambda i: (0, OFF_VC // W_C)),
                  full((t_new * t_new, W_C)), full((t_new, W_C)), full((1, W_C)), full((1, W_C))],
        out_specs=[full((n, W_C)), full((n, W_C))],
        out_shape=[jax.ShapeDtypeStruct((n, W_C), BF16), jax.ShapeDtypeStruct((n, W_C), F32)],
        compiler_params=_params(("arbitrary",)), name="gmlp_sample")(
            zs, zs, coef, bias, g.reshape(1, W_C), b.reshape(1, W_C))


def _out_proj_body(ma_ref, mb_ref, mc_ref, h_ref, w_ref, g_ref, b_ref, o_ref, *, alpha):
    mixed = jnp.concatenate([ma_ref[...].astype(BF16), mb_ref[...].astype(BF16),
                             mc_ref[...].astype(BF16)], axis=1)
    y = _dot(mixed, w_ref[...])
    o_ref[...] = _ln(alpha * h_ref[...] + y, g_ref[...], b_ref[...])


def _out_proj(ma, mb, mc, h, w, g, b, alpha):
    n = h.shape[0]
    tm = min(512, n)
    row = lambda i: (i, 0)
    const = lambda i: (0, 0)
    return pl.pallas_call(
        functools.partial(_out_proj_body, alpha=alpha), grid=(n // tm,),
        in_specs=[pl.BlockSpec((tm, W_ATT), row), pl.BlockSpec((tm, W_ATT), row),
                  pl.BlockSpec((tm, W_C), row), pl.BlockSpec((tm, D_MODEL), row),
                  pl.BlockSpec((D_MODEL, D_MODEL), const),
                  pl.BlockSpec((1, D_MODEL), const), pl.BlockSpec((1, D_MODEL), const)],
        out_specs=pl.BlockSpec((tm, D_MODEL), row),
        out_shape=jax.ShapeDtypeStruct((n, D_MODEL), F32),
        compiler_params=_params(("parallel",)), name="out_proj")(
            ma, mb, mc, h, w, g.reshape(1, D_MODEL), b.reshape(1, D_MODEL))


def _ffn_body(h_ref, w1_ref, w2_ref, g_ref, b_ref, o_ref, acc_ref, *, alpha):
    h = h_ref[...]
    hb = h.astype(BF16)
    step = 512
    for c in range(0, D_FF, step):
        a = _dot(hb, w1_ref[:, c:c + step])
        a = jnp.square(jnp.maximum(a, 0.0)).astype(BF16)
        y = _dot(a, w2_ref[c:c + step, :])
        if c == 0:
            acc_ref[...] = y
        else:
            acc_ref[...] += y
    o_ref[...] = _ln(alpha * h + acc_ref[...], g_ref[...], b_ref[...])


def _ffn(h, w1, w2, g, b, alpha):
    n = h.shape[0]
    tm = min(512, n)
    row = lambda i: (i, 0)
    const = lambda i: (0, 0)
    return pl.pallas_call(
        functools.partial(_ffn_body, alpha=alpha), grid=(n // tm,),
        in_specs=[pl.BlockSpec((tm, D_MODEL), row),
                  pl.BlockSpec((D_MODEL, D_FF), const, pipeline_mode=pl.Buffered(1)),
                  pl.BlockSpec((D_FF, D_MODEL), const, pipeline_mode=pl.Buffered(1)),
                  pl.BlockSpec((1, D_MODEL), const), pl.BlockSpec((1, D_MODEL), const)],
        out_specs=pl.BlockSpec((tm, D_MODEL), row),
        out_shape=jax.ShapeDtypeStruct((n, D_MODEL), F32),
        scratch_shapes=[pltpu.VMEM((tm, D_MODEL), F32)],
        compiler_params=_params(("parallel",)), name="ffn")(
            h, w1, w2, g.reshape(1, D_MODEL), b.reshape(1, D_MODEL))


PAGES_PER_BLK = MOBA_BLOCK // PAGE
B_SLOTS = 3


def _expand_heads(q):
    r = lax.broadcasted_iota(jnp.int32, (8, W_ATT), 0)
    c = lax.broadcasted_iota(jnp.int32, (8, W_ATT), 1) // HEAD_DIM
    pieces = [jnp.where(r == c, jnp.broadcast_to(row, (8, W_ATT)), 0.0) for row in q]
    return jnp.concatenate(pieces, axis=0)


def _reduce_heads(o, out_ref, b):
    r = lax.broadcasted_iota(jnp.int32, (8, W_ATT), 0)
    c = lax.broadcasted_iota(jnp.int32, (8, W_ATT), 1) // HEAD_DIM
    for t in range(o.shape[0] // 8):
        piece = jnp.where(r == c, o[8 * t:8 * t + 8, :], 0.0)
        out_ref[t, pl.ds(b, 1), :] = jnp.sum(piece, axis=0, keepdims=True)


def _sample_attn_body(pt_ref, zs_ref, cka, cva, ckb, cvb, oa_ref, ob_ref,
                      ka_buf, va_buf, kb_buf, vb_buf, sem_a, sem_b,
                      *, layer, bsz, n_pages, t_new, slopes):
    b = pl.program_id(0)
    past_len = n_pages * PAGE
    n_blk = n_pages // PAGES_PER_BLK

    def blk_copies(src_k, src_v, dst_k, dst_v, sem, first_page, slot):
        cps = []
        for k in range(PAGES_PER_BLK):
            pg = pt_ref[first_page + k]
            win = pl.ds(k * PAGE, PAGE)
            cps.append(pltpu.make_async_copy(src_k.at[layer, pg], dst_k.at[slot, :, win],
                                             sem.at[slot, 2 * k]))
            cps.append(pltpu.make_async_copy(src_v.at[layer, pg], dst_v.at[slot, :, win],
                                             sem.at[slot, 2 * k + 1]))
        return cps

    def a_copies(seq):
        cps = []
        for k in range(n_blk):
            cps += blk_copies(cka, cva, ka_buf, va_buf, sem_a,
                              seq * n_pages + k * PAGES_PER_BLK, (seq % 2) * n_blk + k)
        return cps

    def b_copies(seq, back, slot):
        return blk_copies(ckb, cvb, kb_buf, vb_buf, sem_b,
                          seq * n_pages + (n_blk - 1 - back) * PAGES_PER_BLK, slot)

    @pl.when(b == 0)
    def _prime():
        for cp in a_copies(0) + b_copies(0, 0, 0):
            cp.start()

    for cp in a_copies(b) + b_copies(b, 0, b % 2):
        cp.wait()

    @pl.when(b + 1 < bsz)
    def _prefetch_next():
        for cp in a_copies(b + 1) + b_copies(b + 1, 0, (b + 1) % 2):
            cp.start()

    scale = HEAD_DIM ** -0.5
    rows = 8 * t_new
    ri = lax.broadcasted_iota(jnp.int32, (rows, 1), 0)
    tok = ri // 8
    hd = ri % 8
    slope = jnp.zeros((rows, 1), F32)
    for h in range(N_HEADS):
        slope = jnp.where(hd == h, float(slopes[h]), slope)

    def seg(t, off):
        return zs_ref[t, pl.ds(b, 1), off:off + W_ATT]

    qa = _expand_heads([seg(t, OFF_QA) * scale for t in range(t_new)])
    qb = _expand_heads([seg(t, OFF_QB) * scale for t in range(t_new)])
    upper2 = _upper2(MOBA_BLOCK)

    sa, ys = [], []
    for s in range(t_new):
        sc = jnp.sum(qa * seg(s, OFF_KA), axis=1, keepdims=True)
        sc = sc - slope * (tok - s).astype(F32)
        sa.append(jnp.where(tok >= s, sc, -jnp.inf))
        ys.append(jnp.sum(qb * seg(s, OFF_KB), axis=1, keepdims=True) * LOG2E)
    m = sa[0]
    for s in range(1, t_new):
        m = jnp.maximum(m, sa[s])
    l = jnp.zeros((rows, 1), F32)
    acc = jnp.zeros((rows, W_ATT), F32)
    for s in range(t_new):
        pr = jnp.exp(sa[s] - m)
        l = l + pr
        acc = acc + pr * seg(s, OFF_VA)
    parts = [(m, l, acc)]
    carry = jnp.zeros((rows, 1), F32)
    accb = jnp.zeros((rows, W_ATT), F32)
    for s in range(t_new - 1, -1, -1):
        valid = tok > s
        sp = _softplus2(ys[s])
        a = jnp.where(valid, jnp.exp2(ys[s] - sp + carry), 0.0)
        accb = accb + a * seg(s, OFF_VB)
        carry = carry - jnp.where(valid, sp, 0.0)
    qa_b = qa.astype(BF16)
    qb_b = qb.astype(BF16)
    kcol = lax.broadcasted_iota(jnp.int32, (rows, MOBA_BLOCK), 1)

    gates = []
    for blk in range(n_blk):
        slot = (b % 2) * n_blk + blk
        raw = _dot(qa_b, ka_buf[slot].astype(BF16))
        gates.append(jnp.sum(raw, axis=1, keepdims=True))
        dist = (past_len + tok - blk * MOBA_BLOCK - kcol).astype(F32)
        sc = raw - slope * dist
        m = jnp.max(sc, axis=1, keepdims=True)
        pr = jnp.exp(sc - m)
        parts.append((m, jnp.sum(pr, axis=1, keepdims=True),
                      _dot_nt(pr.astype(BF16), va_buf[slot].astype(BF16))))

    def sb_block(slot, carry, accb):
        y = _dot(qb_b, kb_buf[slot].astype(BF16)) * LOG2E
        sp = _softplus2(y)
        lk = -sp
        aft = _suffix_sums(lk, upper2)
        a = jnp.exp2(y - sp + aft + carry)
        accb = accb + _dot_nt(a.astype(BF16), vb_buf[slot].astype(BF16))
        return carry + aft[:, 0:1] + lk[:, 0:1], accb

    def alive(carry):
        return (jnp.max(carry) > SB_EXIT2).astype(jnp.int32)

    carry, accb = sb_block(b % 2, carry, accb)

    def older_cond(c):
        return (c[0] < n_blk) & (c[1] > 0)

    def older_body(c):
        cps = b_copies(b, c[0], B_SLOTS - 1)
        for cp in cps:
            cp.start()
        for cp in cps:
            cp.wait()
        carry, accb = sb_block(B_SLOTS - 1, c[2], c[3])
        return c[0] + 1, alive(carry), carry, accb

    accb = lax.while_loop(older_cond, older_body, (jnp.int32(1), alive(carry), carry, accb))[3]

    sel = []
    for nb in range(n_blk):
        rank = jnp.zeros((rows, 1), jnp.int32)
        for mth in range(n_blk):
            if mth < nb:
                rank += jnp.where(gates[mth] >= gates[nb], 1, 0)
            elif mth > nb:
                rank += jnp.where(gates[mth] > gates[nb], 1, 0)
        sel.append(rank < MOBA_TOPK)
    big = parts[0][0]
    for nb in range(n_blk):
        big = jnp.maximum(big, jnp.where(sel[nb], parts[nb + 1][0], -jnp.inf))
    w = jnp.exp(parts[0][0] - big)
    l = w * parts[0][1]
    acc = w * parts[0][2]
    for nb in range(n_blk):
        m_nb, l_nb, acc_nb = parts[nb + 1]
        w = jnp.where(sel[nb], jnp.exp(jnp.minimum(m_nb - big, 0.0)), 0.0)
        l = l + w * l_nb
        acc = acc + w * acc_nb
    _reduce_heads(acc / l, oa_ref, b)
    _reduce_heads(accb, ob_ref, b)


def _sample_attn(zs3, caches, layer, page_table, slopes):
    t_new, bsz, _ = zs3.shape
    n_pages = page_table.shape[1]
    n_blk = n_pages // PAGES_PER_BLK
    assert n_pages % PAGES_PER_BLK == 0
    hbm = pl.BlockSpec(memory_space=pl.ANY)
    blk_buf = lambda n: pltpu.VMEM((n, W_ATT, MOBA_BLOCK), F32)
    out_spec = pl.BlockSpec((t_new, bsz, W_ATT), lambda b, pt: (0, 0, 0))
    grid_spec = pltpu.PrefetchScalarGridSpec(
        num_scalar_prefetch=1, grid=(bsz,),
        in_specs=[pl.BlockSpec((t_new, bsz, IN_WIDTH), lambda b, pt: (0, 0, 0)),
                  hbm, hbm, hbm, hbm],
        out_specs=[out_spec, out_spec],
        scratch_shapes=[blk_buf(2 * n_blk), blk_buf(2 * n_blk),
                        blk_buf(B_SLOTS), blk_buf(B_SLOTS),
                        pltpu.SemaphoreType.DMA((2 * n_blk, 2 * PAGES_PER_BLK)),
                        pltpu.SemaphoreType.DMA((B_SLOTS, 2 * PAGES_PER_BLK))])
    body = functools.partial(_sample_attn_body, layer=layer, bsz=bsz, n_pages=n_pages,
                             t_new=t_new, slopes=tuple(float(s) for s in slopes))
    return pl.pallas_call(
        body, grid_spec=grid_spec,
        out_shape=[jax.ShapeDtypeStruct((t_new, bsz, W_ATT), F32)] * 2,
        compiler_params=_params(("arbitrary",)), name="sample_attn")(
            page_table.reshape(-1), zs3, *caches)


def _gate_matrix(kmean_t):
    rows = jnp.arange(W_ATT)[:, None] // HEAD_DIM
    cols = jnp.arange(LANES)[None, :] // 16
    tiled = jnp.tile(kmean_t[:, :, :16], (1, 1, LANES // 16))
    return jnp.where(rows == cols, tiled, 0.0)


def kernel(x_prompt, x_sample, cache_k_a, cache_v_a, cache_k_b, cache_v_b, page_table, ln_in_g, ln_in_b, w_in, w_out, w_spatial, b_spatial, ln_c_g, ln_c_b, ln1_g, ln1_b, w_ff1, w_ff2, ln2_g, ln2_b):
    batch, seq, _ = x_prompt.shape
    dec_batch, dec_seq, _ = x_sample.shape
    depth = w_in.shape[0]
    alpha = (2 * depth) ** 0.25
    slopes_np = _alibi_slopes(N_HEADS)
    slopes = jnp.asarray(slopes_np)
    n_blk = seq // MOBA_BLOCK
    assert n_blk <= 16 and seq % 512 == 0 and dec_seq <= CHUNK and dec_batch % 8 == 0
    assert all(math.frexp(float(s))[0] == 0.5 for s in slopes_np)

    w_in_b = w_in.astype(BF16)
    w_q = jnp.concatenate([w_in_b[:, :, OFF_QA:OFF_QA + W_ATT], w_in_b[:, :, OFF_QB:OFF_QB + W_ATT],
                           w_in_b[:, :, OFF_UC:]], axis=2)
    w_kv_t = jnp.swapaxes(jnp.concatenate(
        [w_in_b[:, :, OFF_KA:OFF_KA + 2 * W_ATT], w_in_b[:, :, OFF_KB:OFF_KB + 2 * W_ATT]],
        axis=2), 1, 2)
    w_out_b = w_out.astype(BF16)
    w_ff1_b = w_ff1.astype(BF16)
    w_ff2_b = w_ff2.astype(BF16)
    caches = [jnp.transpose(c, (0, 1, 3, 4, 2)).reshape(c.shape[0], c.shape[1], W_ATT, PAGE)
              for c in (cache_k_a, cache_v_a, cache_k_b, cache_v_b)]

    bias_p = jnp.repeat(jnp.swapaxes(b_spatial, 1, 2), HEAD_DIM, axis=2)
    coef_s = jnp.repeat(jnp.transpose(w_spatial[:, :, :dec_seq, :dec_seq], (0, 2, 3, 1)),
                        HEAD_DIM, axis=3).reshape(depth, dec_seq * dec_seq, W_C)
    bias_s = jnp.repeat(jnp.swapaxes(b_spatial[:, :, :dec_seq], 1, 2), HEAD_DIM, axis=2)

    hp = x_prompt.reshape(batch * seq, D_MODEL)
    hs = jnp.swapaxes(x_sample, 0, 1).reshape(dec_seq * dec_batch, D_MODEL)
    kv_p = [jnp.zeros((depth, batch, W_ATT, seq), F32) for _ in range(4)]
    kv_s = [jnp.zeros((depth, dec_seq, W_ATT, dec_batch), F32) for _ in range(4)]
    vc_s = []
    for l in range(depth):
        ln = (ln_in_g, ln_in_b) if l == 0 else None
        lnc_g, lnc_b = ln_c_g[l].reshape(-1), ln_c_b[l].reshape(-1)

        hp, zb, kv_p, kmean_t = _in_proj_prompt(hp, w_q[l], w_kv_t[l], ln, kv_p, l, depth, batch, seq)
        selb = _moba_gate(zb, _gate_matrix(kmean_t), batch, seq)
        mix_a = _moba_prompt(zb, kv_p[0], kv_p[1], selb, slopes, l, batch, seq)
        mix_b = _sb_prompt(zb, kv_p[2], kv_p[3], l, batch, seq)
        mix_c = _gmlp_prompt(zb, w_spatial[l], bias_p[l], lnc_g, lnc_b)
        h1 = _out_proj(mix_a, mix_b, mix_c, hp, w_out_b[l], ln1_g[l], ln1_b[l], alpha)
        hp = _ffn(h1, w_ff1_b[l], w_ff2_b[l], ln2_g[l], ln2_b[l], alpha)

        hs, zs, kv_s = _in_proj_sample(hs, w_in_b[l], w_kv_t[l], ln, kv_s, l, depth, dec_seq, dec_batch)
        oa, ob = _sample_attn(zs.reshape(dec_seq, dec_batch, IN_WIDTH), caches, l,
                              page_table, slopes_np)
        mix_c, vn = _gmlp_sample(zs, coef_s[l], bias_s[l], lnc_g, lnc_b, dec_seq, dec_batch)
        vc_s.append(vn)
        h1 = _out_proj(oa.reshape(-1, W_ATT), ob.reshape(-1, W_ATT), mix_c, hs,
                       w_out_b[l], ln1_g[l], ln1_b[l], alpha)
        hs = _ffn(h1, w_ff1_b[l], w_ff2_b[l], ln2_g[l], ln2_b[l], alpha)

    kv_out_p = [jnp.transpose(a.reshape(depth, batch, N_HEADS, HEAD_DIM, seq), (0, 1, 4, 2, 3))
                for a in kv_p]
    kv_out_s = [jnp.transpose(a.reshape(depth, dec_seq, N_HEADS, HEAD_DIM, dec_batch), (0, 4, 1, 2, 3))
                for a in kv_s]
    vc_out = jnp.swapaxes(jnp.stack(vc_s).reshape(depth, dec_seq, dec_batch, W_C), 1, 2)
    y_s = jnp.swapaxes(hs.reshape(dec_seq, dec_batch, D_MODEL), 0, 1)
    return (hp.reshape(batch, seq, D_MODEL), y_s, *kv_out_p, *kv_out_s, vc_out)
```

```python
import functools
import math

import jax
import jax.numpy as jnp
import numpy as np
from jax import lax
from jax.experimental import pallas as pl
from jax.experimental.pallas import tpu as pltpu

F32 = jnp.float32
BF16 = jnp.bfloat16

D_MODEL = 1024
HEAD_DIM = 64
N_HEADS = 6
W_ATT = N_HEADS * HEAD_DIM
C_GROUPS = 4
W_C = C_GROUPS * HEAD_DIM
IN_WIDTH = 6 * W_ATT + 2 * W_C
D_FF = 4 * D_MODEL
MOBA_BLOCK = 256
MOBA_TOPK = 3
CHUNK = 128
PAGE = 128
LN_EPS = 1e-5
LANES = 128
NEG_BIG = -1e30
SB_EXIT = -110.0

OFF_QA, OFF_KA, OFF_VA = 0, W_ATT, 2 * W_ATT
OFF_QB, OFF_KB, OFF_VB = 3 * W_ATT, 4 * W_ATT, 5 * W_ATT
OFF_UC, OFF_VC = 6 * W_ATT, 6 * W_ATT + W_C
ZP_QA, ZP_QB, ZP_UC, ZP_VC = 0, W_ATT, 2 * W_ATT, 2 * W_ATT + W_C
ZP_WIDTH = 2 * W_ATT + 2 * W_C

VMEM_LIMIT = 56 * 1024 * 1024


def _alibi_slopes(n):
    def pow2(m):
        start = 2.0 ** (-8.0 / m)
        return [start ** (i + 1) for i in range(m)]
    p = 2 ** int(math.floor(math.log2(n)))
    s = pow2(p)
    if p < n:
        s = s + pow2(2 * p)[0::2][: n - p]
    return np.array(s, dtype=np.float32)


def _ln(x, g, b):
    mu = jnp.mean(x, axis=-1, keepdims=True)
    d = x - mu
    var = jnp.mean(d * d, axis=-1, keepdims=True)
    return d * lax.rsqrt(var + LN_EPS) * g + b


def _dot(a, b, precision=None):
    return jnp.dot(a, b, preferred_element_type=F32, precision=precision)


def _dot_nt(a, b, precision=None):
    return lax.dot_general(a, b, (((1,), (1,)), ((), ())),
                           preferred_element_type=F32, precision=precision)


def _params(sem):
    return pltpu.CompilerParams(dimension_semantics=sem, vmem_limit_bytes=VMEM_LIMIT)


def _in_proj_prompt_body(*refs, apply_ln, tm):
    it = iter(refs)
    x_ref = next(it)
    if apply_ln:
        g_ref, b_ref = next(it), next(it)
    wq_ref, wkv_ref = next(it), next(it)
    for _ in range(4):
        next(it)
    if apply_ln:
        h_ref = next(it)
    zb_ref = next(it)
    kv_refs = [next(it) for _ in range(4)]
    km_ref = next(it)
    ti = pl.program_id(1)

    x = x_ref[...]
    if apply_ln:
        x = _ln(x, g_ref[...], b_ref[...])
        h_ref[...] = x
    hb = x.astype(BF16)
    for c in range(0, ZP_WIDTH, 256):
        zb_ref[:, c:c + 256] = _dot(hb, wq_ref[:, c:c + 256]).astype(BF16)
    lane = lax.broadcasted_iota(jnp.int32, (W_ATT, LANES), 1)

    @pl.when(ti == 0)
    def _zero_means():
        km_ref[...] = jnp.zeros_like(km_ref)

    kt_all = _dot_nt(wkv_ref[...], hb)
    for t in range(4):
        kt = kt_all[t * W_ATT:(t + 1) * W_ATT]
        kv_refs[t][...] = kt
        if t == 0:
            km = km_ref[...]
            for r in range(tm // MOBA_BLOCK):
                mean = jnp.sum(kt[:, r * MOBA_BLOCK:(r + 1) * MOBA_BLOCK], axis=1,
                               keepdims=True) * (1.0 / MOBA_BLOCK)
                km = jnp.where(lane == ti * (tm // MOBA_BLOCK) + r, mean, km)
            km_ref[...] = km


def _in_proj_prompt(x, wq, wkv, ln, kv_prev, layer, depth, batch, seq):
    n = x.shape[0]
    tm = 512
    nt = seq // tm
    row = lambda b, t: (b * nt + t, 0)
    const = lambda b, t: (0, 0)
    args, in_specs = [x], [pl.BlockSpec((tm, D_MODEL), row)]
    if ln is not None:
        args += [ln[0].reshape(1, D_MODEL), ln[1].reshape(1, D_MODEL)]
        in_specs += [pl.BlockSpec((1, D_MODEL), const)] * 2
    args += [wq, wkv]
    in_specs += [pl.BlockSpec((None, D_MODEL, ZP_WIDTH), lambda b, t: (layer, 0, 0)),
                 pl.BlockSpec((None, 4 * W_ATT, D_MODEL), lambda b, t: (layer, 0, 0))]
    out_shape, out_specs = [], []
    if ln is not None:
        out_shape.append(jax.ShapeDtypeStruct((n, D_MODEL), F32))
        out_specs.append(pl.BlockSpec((tm, D_MODEL), row))
    out_shape.append(jax.ShapeDtypeStruct((n, ZP_WIDTH), BF16))
    out_specs.append(pl.BlockSpec((tm, ZP_WIDTH), row))
    aliases = {}
    for t in range(4):
        aliases[len(args)] = len(out_shape)
        args.append(kv_prev[t])
        in_specs.append(pl.BlockSpec(memory_space=pl.ANY))
        out_shape.append(jax.ShapeDtypeStruct((depth, batch, W_ATT, seq), F32))
        out_specs.append(pl.BlockSpec((None, None, W_ATT, tm), lambda b, t: (layer, b, 0, t)))
    out_shape.append(jax.ShapeDtypeStruct((batch, W_ATT, LANES), F32))
    out_specs.append(pl.BlockSpec((None, W_ATT, LANES), lambda b, t: (b, 0, 0)))
    body = functools.partial(_in_proj_prompt_body, apply_ln=ln is not None, tm=tm)
    outs = list(pl.pallas_call(
        body, grid=(batch, nt), in_specs=in_specs, out_specs=out_specs, out_shape=out_shape,
        input_output_aliases=aliases, compiler_params=_params(("parallel", "arbitrary")),
        name="in_proj_prompt")(*args))
    h = outs.pop(0) if ln is not None else x
    return h, outs[0], outs[1:5], outs[5]


def _in_proj_sample_body(*refs, apply_ln, t_new, bsz):
    it = iter(refs)
    x_ref = next(it)
    if apply_ln:
        g_ref, b_ref = next(it), next(it)
    w_ref, wkv_ref = next(it), next(it)
    for _ in range(4):
        next(it)
    if apply_ln:
        h_ref = next(it)
    zs_ref = next(it)
    kv_refs = [next(it) for _ in range(4)]

    x = x_ref[...]
    if apply_ln:
        x = _ln(x, g_ref[...], b_ref[...])
        h_ref[...] = x
    hb = x.astype(BF16)
    for c in range(0, IN_WIDTH, 256):
        zs_ref[:, c:c + 256] = _dot(hb, w_ref[:, c:c + 256])
    for t in range(4):
        kt = _dot_nt(wkv_ref[t * W_ATT:(t + 1) * W_ATT, :], hb)
        for tt in range(t_new):
            kv_refs[t][tt] = kt[:, tt * bsz:(tt + 1) * bsz]


def _in_proj_sample(x, w, wkv, ln, kv_prev, layer, depth, t_new, bsz):
    n = x.shape[0]
    const = lambda i: (0, 0)
    args, in_specs = [x], [pl.BlockSpec((n, D_MODEL), const)]
    if ln is not None:
        args += [ln[0].reshape(1, D_MODEL), ln[1].reshape(1, D_MODEL)]
        in_specs += [pl.BlockSpec((1, D_MODEL), const)] * 2
    args += [w, wkv]
    in_specs += [pl.BlockSpec((None, D_MODEL, IN_WIDTH), lambda i: (layer, 0, 0)),
                 pl.BlockSpec((None, 4 * W_ATT, D_MODEL), lambda i: (layer, 0, 0))]
    out_shape, out_specs = [], []
    if ln is not None:
        out_shape.append(jax.ShapeDtypeStruct((n, D_MODEL), F32))
        out_specs.append(pl.BlockSpec((n, D_MODEL), const))
    out_shape.append(jax.ShapeDtypeStruct((n, IN_WIDTH), F32))
    out_specs.append(pl.BlockSpec((n, IN_WIDTH), const))
    aliases = {}
    for t in range(4):
        aliases[len(args)] = len(out_shape)
        args.append(kv_prev[t])
        in_specs.append(pl.BlockSpec(memory_space=pl.ANY))
        out_shape.append(jax.ShapeDtypeStruct((depth, t_new, W_ATT, bsz), F32))
        out_specs.append(pl.BlockSpec((None, t_new, W_ATT, bsz), lambda i: (layer, 0, 0, 0)))
    body = functools.partial(_in_proj_sample_body, apply_ln=ln is not None, t_new=t_new, bsz=bsz)
    outs = list(pl.pallas_call(
        body, grid=(1,), in_specs=in_specs, out_specs=out_specs, out_shape=out_shape,
        input_output_aliases=aliases, compiler_params=_params(("arbitrary",)),
        name="in_proj_sample")(*args))
    h = outs.pop(0) if ln is not None else x
    return h, outs[0], outs[1:5]


def _gate_body(q_ref, kmt_ref, o_ref):
    cur = pl.program_id(1)
    g = _dot_nt(kmt_ref[...], q_ref[...].astype(F32), precision=lax.Precision.HIGHEST)
    blk = lax.broadcasted_iota(jnp.int32, g.shape, 0) & 15
    past = blk < cur
    g = jnp.where(past, g, -jnp.inf)
    rank = jnp.zeros(g.shape, jnp.int32)
    for s in range(1, 16):
        lower = pltpu.roll(g, s, axis=0)
        rank += jnp.where(lower >= g, jnp.where(blk >= s, 1, 0), 0)
        upper = pltpu.roll(g, LANES - s, axis=0)
        rank += jnp.where(upper > g, jnp.where(blk + s <= 15, 1, 0), 0)
    sel = jnp.where(past, jnp.where(rank < MOBA_TOPK, 0.0, NEG_BIG), NEG_BIG)
    o_ref[...] = sel.T


def _moba_gate(zb, kmt, batch, seq):
    nq = seq // MOBA_BLOCK
    return pl.pallas_call(
        _gate_body, grid=(batch, nq),
        in_specs=[pl.BlockSpec((MOBA_BLOCK, W_ATT), lambda b, i: (b * nq + i, ZP_QA // W_ATT)),
                  pl.BlockSpec((None, LANES, W_ATT), lambda b, i: (b, 0, 0))],
        out_specs=pl.BlockSpec((MOBA_BLOCK, LANES), lambda b, i: (b * nq + i, 0)),
        out_shape=jax.ShapeDtypeStruct((batch * seq, LANES), F32),
        compiler_params=_params(("parallel", "parallel")), name="moba_gate")(zb, kmt)


ROLE_SEL0, ROLE_SEL1 = 0, 1
ROLE_ROW, ROLE_KEY = 96, 97
ROLE_BLK, ROLE_2ND = 98, 99


ATT_PAIRS = 3


def _moba_body(slopes_ref, q_ref, kt_ref, vt_ref, selb_ref, o_ref, lhs_s, rhs_s):
    pg = pl.program_id(1)
    i = pl.program_id(2)
    tq = MOBA_BLOCK
    selb = selb_ref[...]
    lane = lax.broadcasted_iota(jnp.int32, (2 * tq, LANES), 1)
    row2 = lax.broadcasted_iota(jnp.int32, (2 * tq, LANES), 0)
    head1 = row2 >= tq
    rowf = jnp.where(head1, row2 - tq, row2).astype(F32)
    role = lax.broadcasted_iota(jnp.int32, (LANES, 2 * tq), 0)
    key2 = lax.broadcasted_iota(jnp.int32, (LANES, 2 * tq), 1)
    second = key2 >= tq
    keyf = jnp.where(second, key2 - tq, key2).astype(F32)
    rhs_roles = jnp.where(
        (role == ROLE_SEL0) & jnp.logical_not(second), 1.0,
        jnp.where(((role == ROLE_SEL1) | (role == ROLE_2ND)) & second, 1.0,
                  jnp.where((role == ROLE_ROW) | (role == ROLE_BLK), 1.0,
                            jnp.where(role == ROLE_KEY, keyf, 0.0)))).astype(BF16)
    rq = jnp.where(lax.broadcasted_iota(jnp.int32, (2 * tq, tq), 0) >= tq,
                   lax.broadcasted_iota(jnp.int32, (2 * tq, tq), 0) - tq,
                   lax.broadcasted_iota(jnp.int32, (2 * tq, tq), 0))
    ck = lax.broadcasted_iota(jnp.int32, (2 * tq, tq), 1)
    d0 = (rq - ck).astype(F32)
    causal = ck <= rq
    row0 = pl.multiple_of(i * tq, tq)
    fixed = (lane == ROLE_ROW) | (lane == ROLE_KEY) | (lane == ROLE_2ND)

    pairs = []
    for pr_i in range(ATT_PAIRS):
        lanes = slice(pr_i * LANES, (pr_i + 1) * LANES)
        rhs_s[pr_i, LANES:2 * LANES, :] = rhs_roles
        q = q_ref[:, lanes].astype(F32) * (HEAD_DIM ** -0.5)
        q2 = jnp.concatenate([q, q], axis=0)
        qh = jnp.where((lane >= HEAD_DIM) == head1, q2, 0.0).astype(BF16)
        lhs_s[pr_i, :, 0:LANES] = qh
        hd0 = 2 * (pg * ATT_PAIRS + pr_i)
        slope = jnp.where(lax.broadcasted_iota(jnp.int32, (2 * tq, 1), 0) >= tq,
                          slopes_ref[hd0 + 1], slopes_ref[hd0])
        lhs_fix = jnp.where(lane == ROLE_ROW, -slope * rowf,
                            jnp.where(lane == ROLE_KEY, slope, slope * tq))
        pairs.append((hd0, slope, lhs_fix, qh))

    def own_block():
        state = []
        for pr_i in range(ATT_PAIRS):
            lanes = slice(pr_i * LANES, (pr_i + 1) * LANES)
            _, slope, _, qh = pairs[pr_i]
            kd = kt_ref[lanes, pl.ds(row0, tq)].astype(BF16)
            vd = vt_ref[lanes, pl.ds(row0, tq)].astype(BF16)
            s = jnp.where(causal, _dot(qh, kd) - slope * d0, -jnp.inf)
            m = jnp.max(s, axis=1, keepdims=True)
            pr = jnp.exp(s - m)
            state += [m, jnp.sum(pr, axis=1, keepdims=True), _dot_nt(pr.astype(BF16), vd)]
        return tuple(state)

    def body(pp, carry):
        j0 = 2 * pp
        r0 = pl.multiple_of(j0 * tq, 2 * tq)
        out = []
        for pr_i in range(ATT_PAIRS):
            lanes = slice(pr_i * LANES, (pr_i + 1) * LANES)
            hd0, slope, lhs_fix, _ = pairs[pr_i]
            m, l, acc = carry[3 * pr_i:3 * pr_i + 3]
            rhs_s[pr_i, 0:LANES, :] = kt_ref[lanes, pl.ds(r0, 2 * tq)].astype(BF16)
            v2 = vt_ref[lanes, pl.ds(r0, 2 * tq)].astype(BF16)
            sel = jnp.concatenate(
                [pltpu.roll(selb, (LANES - (hd0 + hh) * 16 - j0) % LANES, axis=1) for hh in range(2)],
                axis=0)
            off = -slope * ((i - j0) * tq).astype(F32)
            ext = jnp.where(fixed, lhs_fix, jnp.where(lane == ROLE_BLK, off, sel))
            lhs_s[pr_i, :, LANES:2 * LANES] = ext.astype(BF16)
            t = _dot(lhs_s[pr_i], rhs_s[pr_i])
            m_new = jnp.maximum(m, jnp.max(t, axis=1, keepdims=True))
            alpha = jnp.exp(m - m_new)
            pr = jnp.exp(t - m_new)
            l = alpha * l + jnp.sum(pr, axis=1, keepdims=True)
            acc = alpha * acc + _dot_nt(pr.astype(BF16), v2)
            out += [m_new, l, acc]
        return tuple(out)

    state = lax.fori_loop(0, (i + 1) // 2, body, own_block())
    for pr_i in range(ATT_PAIRS):
        o = state[3 * pr_i + 2] / state[3 * pr_i + 1]
        o_ref[:, pr_i * LANES:(pr_i + 1) * LANES] = jnp.where(
            lax.broadcasted_iota(jnp.int32, (tq, LANES), 1) < HEAD_DIM, o[:tq], o[tq:]).astype(o_ref.dtype)


def _moba_prompt(zb, kt, vt, selb, slopes, layer, batch, seq):
    nq = seq // MOBA_BLOCK
    wq = ATT_PAIRS * LANES
    cq = ZP_QA // wq
    kv_spec = pl.BlockSpec((None, None, wq, seq), lambda b, p, i, s: (layer, b, p, 0))
    grid_spec = pltpu.PrefetchScalarGridSpec(
        num_scalar_prefetch=1, grid=(batch, N_HEADS // 2 // ATT_PAIRS, nq),
        in_specs=[pl.BlockSpec((MOBA_BLOCK, wq), lambda b, p, i, s: (b * nq + i, cq + p)),
                  kv_spec, kv_spec,
                  pl.BlockSpec((MOBA_BLOCK, LANES), lambda b, p, i, s: (b * nq + i, 0))],
        out_specs=pl.BlockSpec((MOBA_BLOCK, wq), lambda b, p, i, s: (b * nq + i, p)),
        scratch_shapes=[pltpu.VMEM((ATT_PAIRS, 2 * MOBA_BLOCK, 2 * LANES), BF16),
                        pltpu.VMEM((ATT_PAIRS, 2 * LANES, 2 * MOBA_BLOCK), BF16)])
    return pl.pallas_call(
        _moba_body, grid_spec=grid_spec,
        out_shape=jax.ShapeDtypeStruct((batch * seq, W_ATT), BF16),
        compiler_params=_params(("parallel", "parallel", "arbitrary")),
        name="moba_prompt")(slopes, zb, kt, vt, selb)


LOG2E = 1.4426950408889634
SB_EXIT2 = SB_EXIT * LOG2E


def _softplus2(y):
    return jnp.maximum(y, 0.0) + jnp.log2(1.0 + jnp.exp2(-jnp.abs(y)))


def _suffix_sums(lk, upper2):
    hi = lk.astype(BF16)
    lo = (lk - hi.astype(F32)).astype(BF16)
    return _dot(jnp.concatenate([hi, lo], axis=1), upper2)


def _upper2(n):
    r = lax.broadcasted_iota(jnp.int32, (2 * n, n), 0)
    c = lax.broadcasted_iota(jnp.int32, (2 * n, n), 1)
    return jnp.where(jnp.where(r >= n, r - n, r) > c, 1.0, 0.0).astype(BF16)


def _sb_body(q_ref, kt_ref, vt_ref, o_ref):
    i = pl.program_id(2)
    tq = MOBA_BLOCK
    lane = lax.broadcasted_iota(jnp.int32, (2 * tq, LANES), 1)
    row2 = lax.broadcasted_iota(jnp.int32, (2 * tq, tq), 0)
    rq = jnp.where(row2 >= tq, row2 - tq, row2)
    ck = lax.broadcasted_iota(jnp.int32, (2 * tq, tq), 1)
    before = ck < rq
    upper2 = _upper2(tq)
    head1 = lax.broadcasted_iota(jnp.int32, (2 * tq, LANES), 0) >= tq
    qhs = []
    for pr_i in range(ATT_PAIRS):
        q = q_ref[:, pr_i * LANES:(pr_i + 1) * LANES].astype(F32) * (HEAD_DIM ** -0.5)
        q2 = jnp.concatenate([q, q], axis=0)
        qhs.append(jnp.where((lane >= HEAD_DIM) == head1, q2, 0.0).astype(BF16))

    def block(j, st, diag):
        r0 = pl.multiple_of(j * tq, tq)
        ys, sps, lks, vbs = [], [], [], []
        for pr_i in range(ATT_PAIRS):
            lanes = slice(pr_i * LANES, (pr_i + 1) * LANES)
            kb = kt_ref[lanes, pl.ds(r0, tq)].astype(BF16)
            vbs.append(vt_ref[lanes, pl.ds(r0, tq)].astype(BF16))
            y = _dot(qhs[pr_i], kb) * LOG2E
            sp = _softplus2(y)
            lk = -sp
            if diag:
                lk = jnp.where(before, lk, 0.0)
            ys.append(y)
            sps.append(sp)
            lks.append(lk)
        aft_all = _suffix_sums(jnp.concatenate(lks, axis=0), upper2)
        out = []
        for pr_i in range(ATT_PAIRS):
            carry, acc = st[2 * pr_i], st[2 * pr_i + 1]
            aft = aft_all[pr_i * 2 * tq:(pr_i + 1) * 2 * tq]
            a = jnp.exp2(ys[pr_i] - sps[pr_i] + aft + carry)
            if diag:
                a = jnp.where(before, a, 0.0)
            out += [carry + aft[:, 0:1] + lks[pr_i][:, 0:1],
                    acc + _dot_nt(a.astype(BF16), vbs[pr_i])]
        return tuple(out)

    def alive(st):
        top = st[0]
        for pr_i in range(1, ATT_PAIRS):
            top = jnp.maximum(top, st[2 * pr_i])
        return (jnp.max(top) > SB_EXIT2).astype(jnp.int32)

    zc, za = jnp.zeros((2 * tq, 1), F32), jnp.zeros((2 * tq, LANES), F32)
    init = (zc, za) * ATT_PAIRS
    st = lax.cond(i >= 1,
                  lambda: block(i - 1, block(i, init, True), False),
                  lambda: block(i, init, True))

    def cond(c):
        return (c[0] >= 0) & (c[1] > 0)

    def body(c):
        st = block(c[0], c[2:], False)
        return (c[0] - 1, alive(st)) + st

    res = lax.while_loop(cond, body, (i - 2, alive(st)) + st)
    for pr_i in range(ATT_PAIRS):
        acc = res[2 + 2 * pr_i + 1]
        o_ref[:, pr_i * LANES:(pr_i + 1) * LANES] = jnp.where(
            lax.broadcasted_iota(jnp.int32, (tq, LANES), 1) < HEAD_DIM, acc[:tq], acc[tq:]).astype(o_ref.dtype)


def _sb_prompt(zb, kt, vt, layer, batch, seq):
    nq = seq // MOBA_BLOCK
    wq = ATT_PAIRS * LANES
    cq = ZP_QB // wq
    kv_spec = pl.BlockSpec((None, None, wq, seq), lambda b, p, i: (layer, b, p, 0))
    return pl.pallas_call(
        _sb_body, grid=(batch, N_HEADS // 2 // ATT_PAIRS, nq),
        in_specs=[pl.BlockSpec((MOBA_BLOCK, wq), lambda b, p, i: (b * nq + i, cq + p)),
                  kv_spec, kv_spec],
        out_specs=pl.BlockSpec((MOBA_BLOCK, wq), lambda b, p, i: (b * nq + i, p)),
        out_shape=jax.ShapeDtypeStruct((batch * seq, W_ATT), BF16),
        compiler_params=_params(("parallel", "parallel", "arbitrary")),
        name="sb_prompt")(zb, kt, vt)


def _group_norm(v, g, b):
    hp = lax.Precision.HIGHEST
    gi = lax.broadcasted_iota(jnp.int32, (W_C, W_C), 0) // HEAD_DIM
    gj = lax.broadcasted_iota(jnp.int32, (W_C, W_C), 1) // HEAD_DIM
    gmean = jnp.where(gi == gj, 1.0 / HEAD_DIM, 0.0).astype(F32)
    mu = _dot(v, gmean, precision=hp)
    d = v - mu
    var = _dot(d * d, gmean, precision=hp)
    return d * lax.rsqrt(var + LN_EPS) * g + b


def _gmlp_prompt_body(u_ref, v_ref, ws_ref, bias_ref, g_ref, b_ref, o_ref, *, rows):
    ti = lax.broadcasted_iota(jnp.int32, (CHUNK, CHUNK), 0)
    tj = lax.broadcasted_iota(jnp.int32, (CHUNK, CHUNK), 1)
    lane = lax.broadcasted_iota(jnp.int32, (CHUNK, LANES), 1)
    wpair = []
    for pr in range(2):
        w0 = jnp.where(tj <= ti, ws_ref[2 * pr], 0.0)
        w1 = jnp.where(tj <= ti, ws_ref[2 * pr + 1], 0.0)
        wpair.append(jnp.concatenate([w0, w1], axis=1).astype(BF16))
    for r in range(rows // CHUNK):
        sl = slice(r * CHUNK, (r + 1) * CHUNK)
        u = jax.nn.gelu(u_ref[sl, :].astype(F32))
        vn = _group_norm(jax.nn.gelu(v_ref[sl, :].astype(F32)), g_ref[...], b_ref[...])
        mixes = []
        for pr in range(2):
            vp = vn[:, pr * LANES:(pr + 1) * LANES]
            rhs = jnp.concatenate([jnp.where(lane < HEAD_DIM, vp, 0.0),
                                   jnp.where(lane >= HEAD_DIM, vp, 0.0)], axis=0).astype(BF16)
            mixes.append(_dot(wpair[pr], rhs))
        mix = jnp.concatenate(mixes, axis=1) + bias_ref[...]
        o_ref[sl, :] = (u * mix).astype(o_ref.dtype)


def _gmlp_prompt(zb, ws, bias, g, b):
    n = zb.shape[0]
    rows = 512
    return pl.pallas_call(
        functools.partial(_gmlp_prompt_body, rows=rows), grid=(n // rows,),
        in_specs=[pl.BlockSpec((rows, W_C), lambda i: (i, ZP_UC // W_C)),
                  pl.BlockSpec((rows, W_C), lambda i: (i, ZP_VC // W_C)),
                  pl.BlockSpec((C_GROUPS, CHUNK, CHUNK), lambda i: (0, 0, 0)),
                  pl.BlockSpec((CHUNK, W_C), lambda i: (0, 0)),
                  pl.BlockSpec((1, W_C), lambda i: (0, 0)),
                  pl.BlockSpec((1, W_C), lambda i: (0, 0))],
        out_specs=pl.BlockSpec((rows, W_C), lambda i: (i, 0)),
        out_shape=jax.ShapeDtypeStruct((n, W_C), BF16),
        compiler_params=_params(("parallel",)), name="gmlp_prompt")(
            zb, zb, ws, bias, g.reshape(1, W_C), b.reshape(1, W_C))


def _gmlp_sample_body(u_ref, v_ref, coef_ref, bias_ref, g_ref, b_ref, o_ref, vn_ref, *, t_new, bsz):
    vns = []
    for t in range(t_new):
        sl = slice(t * bsz, (t + 1) * bsz)
        vn = _group_norm(jax.nn.gelu(v_ref[sl, :]), g_ref[...], b_ref[...])
        vn_ref[sl, :] = vn
        vns.append(vn)
    for t in range(t_new):
        sl = slice(t * bsz, (t + 1) * bsz)
        mix = bias_ref[t:t + 1, :]
        for s in range(t + 1):
            mix = mix + coef_ref[t * t_new + s:t * t_new + s + 1, :] * vns[s]
        o_ref[sl, :] = (jax.nn.gelu(u_ref[sl, :]) * mix).astype(o_ref.dtype)


def _gmlp_sample(zs, coef, bias, g, b, t_new, bsz):
    n = zs.shape[0]
    full = lambda shape: pl.BlockSpec(shape, lambda i: (0, 0))
    return pl.pallas_call(
        functools.partial(_gmlp_sample_body, t_new=t_new, bsz=bsz), grid=(1,),
        in_specs=[pl.BlockSpec((n, W_C), lambda i: (0, OFF_UC // W_C)),
                  pl.BlockSpec((n, W_C), lambda i: (0, OFF_VC // W_C)),
                  full((t_new * t_new, W_C)), full((t_new, W_C)), full((1, W_C)), full((1, W_C))],
        out_specs=[full((n, W_C)), full((n, W_C))],
        out_shape=[jax.ShapeDtypeStruct((n, W_C), BF16), jax.ShapeDtypeStruct((n, W_C), F32)],
        compiler_params=_params(("arbitrary",)), name="gmlp_sample")(
            zs, zs, coef, bias, g.reshape(1, W_C), b.reshape(1, W_C))


def _out_proj_body(ma_ref, mb_ref, mc_ref, h_ref, w_ref, g_ref, b_ref, o_ref, *, alpha):
    mixed = jnp.concatenate([ma_ref[...].astype(BF16), mb_ref[...].astype(BF16),
                             mc_ref[...].astype(BF16)], axis=1)
    y = _dot(mixed, w_ref[...])
    o_ref[...] = _ln(alpha * h_ref[...] + y, g_ref[...], b_ref[...])


def _out_proj(ma, mb, mc, h, w, layer, g, b, alpha):
    n = h.shape[0]
    tm = min(512, n)
    row = lambda i: (i, 0)
    const = lambda i: (0, 0)
    return pl.pallas_call(
        functools.partial(_out_proj_body, alpha=alpha), grid=(n // tm,),
        in_specs=[pl.BlockSpec((tm, W_ATT), row), pl.BlockSpec((tm, W_ATT), row),
                  pl.BlockSpec((tm, W_C), row), pl.BlockSpec((tm, D_MODEL), row),
                  pl.BlockSpec((None, D_MODEL, D_MODEL), lambda i: (layer, 0, 0)),
                  pl.BlockSpec((1, D_MODEL), const), pl.BlockSpec((1, D_MODEL), const)],
        out_specs=pl.BlockSpec((tm, D_MODEL), row),
        out_shape=jax.ShapeDtypeStruct((n, D_MODEL), F32),
        compiler_params=_params(("parallel",)), name="out_proj")(
            ma, mb, mc, h, w, g.reshape(1, D_MODEL), b.reshape(1, D_MODEL))


def _ffn_body(h_ref, w1_ref, w2_ref, g_ref, b_ref, o_ref, acc_ref, *, alpha):
    h = h_ref[...]
    hb = h.astype(BF16)
    step = 512
    for c in range(0, D_FF, step):
        a = _dot(hb, w1_ref[:, c:c + step])
        a = jnp.square(jnp.maximum(a, 0.0)).astype(BF16)
        y = _dot(a, w2_ref[c:c + step, :])
        if c == 0:
            acc_ref[...] = y
        else:
            acc_ref[...] += y
    o_ref[...] = _ln(alpha * h + acc_ref[...], g_ref[...], b_ref[...])


def _ffn(h, w1, w2, layer, g, b, alpha):
    n = h.shape[0]
    tm = min(512, n)
    row = lambda i: (i, 0)
    const = lambda i: (0, 0)
    return pl.pallas_call(
        functools.partial(_ffn_body, alpha=alpha), grid=(n // tm,),
        in_specs=[pl.BlockSpec((tm, D_MODEL), row),
                  pl.BlockSpec((None, D_MODEL, D_FF), lambda i: (layer, 0, 0),
                               pipeline_mode=pl.Buffered(1)),
                  pl.BlockSpec((None, D_FF, D_MODEL), lambda i: (layer, 0, 0),
                               pipeline_mode=pl.Buffered(1)),
                  pl.BlockSpec((1, D_MODEL), const), pl.BlockSpec((1, D_MODEL), const)],
        out_specs=pl.BlockSpec((tm, D_MODEL), row),
        out_shape=jax.ShapeDtypeStruct((n, D_MODEL), F32),
        scratch_shapes=[pltpu.VMEM((tm, D_MODEL), F32)],
        compiler_params=_params(("parallel",)), name="ffn")(
            h, w1, w2, g.reshape(1, D_MODEL), b.reshape(1, D_MODEL))


PAGES_PER_BLK = MOBA_BLOCK // PAGE
B_SLOTS = 3


def _expand_heads(q):
    r = lax.broadcasted_iota(jnp.int32, (8, W_ATT), 0)
    c = lax.broadcasted_iota(jnp.int32, (8, W_ATT), 1) // HEAD_DIM
    pieces = [jnp.where(r == c, jnp.broadcast_to(row, (8, W_ATT)), 0.0) for row in q]
    return jnp.concatenate(pieces, axis=0)


def _reduce_heads(o, out_ref, b):
    r = lax.broadcasted_iota(jnp.int32, (8, W_ATT), 0)
    c = lax.broadcasted_iota(jnp.int32, (8, W_ATT), 1) // HEAD_DIM
    for t in range(o.shape[0] // 8):
        piece = jnp.where(r == c, o[8 * t:8 * t + 8, :], 0.0)
        out_ref[t, pl.ds(b, 1), :] = jnp.sum(piece, axis=0, keepdims=True)


def _sample_attn_body(pt_ref, zs_ref, cka, cva, ckb, cvb, oa_ref, ob_ref,
                      ka_buf, va_buf, kb_buf, vb_buf, sem_a, sem_b,
                      *, layer, bsz, n_pages, t_new, slopes):
    b = pl.program_id(0)
    past_len = n_pages * PAGE
    n_blk = n_pages // PAGES_PER_BLK

    def blk_copies(src_k, src_v, dst_k, dst_v, sem, first_page, slot):
        cps = []
        for k in range(PAGES_PER_BLK):
            pg = pt_ref[first_page + k]
            win = pl.ds(k * PAGE, PAGE)
            cps.append(pltpu.make_async_copy(src_k.at[layer, pg], dst_k.at[slot, :, win],
                                             sem.at[slot, 2 * k]))
            cps.append(pltpu.make_async_copy(src_v.at[layer, pg], dst_v.at[slot, :, win],
                                             sem.at[slot, 2 * k + 1]))
        return cps

    def a_copies(seq):
        cps = []
        for k in range(n_blk):
            cps += blk_copies(cka, cva, ka_buf, va_buf, sem_a,
                              seq * n_pages + k * PAGES_PER_BLK, (seq % 2) * n_blk + k)
        return cps

    def b_copies(seq, back, slot):
        return blk_copies(ckb, cvb, kb_buf, vb_buf, sem_b,
                          seq * n_pages + (n_blk - 1 - back) * PAGES_PER_BLK, slot)

    @pl.when(b == 0)
    def _prime():
        for cp in a_copies(0) + b_copies(0, 0, 0):
            cp.start()

    for cp in a_copies(b) + b_copies(b, 0, b % 2):
        cp.wait()

    @pl.when(b + 1 < bsz)
    def _prefetch_next():
        for cp in a_copies(b + 1) + b_copies(b + 1, 0, (b + 1) % 2):
            cp.start()

    scale = HEAD_DIM ** -0.5
    rows = 8 * t_new
    ri = lax.broadcasted_iota(jnp.int32, (rows, 1), 0)
    tok = ri // 8
    hd = ri % 8
    slope = jnp.zeros((rows, 1), F32)
    for h in range(N_HEADS):
        slope = jnp.where(hd == h, float(slopes[h]), slope)

    def seg(t, off):
        return zs_ref[t, pl.ds(b, 1), off:off + W_ATT]

    qa = _expand_heads([seg(t, OFF_QA) * scale for t in range(t_new)])
    qb = _expand_heads([seg(t, OFF_QB) * scale for t in range(t_new)])
    upper2 = _upper2(MOBA_BLOCK)

    sa, ys = [], []
    for s in range(t_new):
        sc = jnp.sum(qa * seg(s, OFF_KA), axis=1, keepdims=True)
        sc = sc - slope * (tok - s).astype(F32)
        sa.append(jnp.where(tok >= s, sc, -jnp.inf))
        ys.append(jnp.sum(qb * seg(s, OFF_KB), axis=1, keepdims=True) * LOG2E)
    m = sa[0]
    for s in range(1, t_new):
        m = jnp.maximum(m, sa[s])
    l = jnp.zeros((rows, 1), F32)
    acc = jnp.zeros((rows, W_ATT), F32)
    for s in range(t_new):
        pr = jnp.exp(sa[s] - m)
        l = l + pr
        acc = acc + pr * seg(s, OFF_VA)
    parts = [(m, l, acc)]
    carry = jnp.zeros((rows, 1), F32)
    accb = jnp.zeros((rows, W_ATT), F32)
    for s in range(t_new - 1, -1, -1):
        valid = tok > s
        sp = _softplus2(ys[s])
        a = jnp.where(valid, jnp.exp2(ys[s] - sp + carry), 0.0)
        accb = accb + a * seg(s, OFF_VB)
        carry = carry - jnp.where(valid, sp, 0.0)
    qa_b = qa.astype(BF16)
    qb_b = qb.astype(BF16)
    kcol = lax.broadcasted_iota(jnp.int32, (rows, MOBA_BLOCK), 1)

    gates = []
    for blk in range(n_blk):
        slot = (b % 2) * n_blk + blk
        raw = _dot(qa_b, ka_buf[slot].astype(BF16))
        gates.append(jnp.sum(raw, axis=1, keepdims=True))
        dist = (past_len + tok - blk * MOBA_BLOCK - kcol).astype(F32)
        sc = raw - slope * dist
        m = jnp.max(sc, axis=1, keepdims=True)
        pr = jnp.exp(sc - m)
        parts.append((m, jnp.sum(pr, axis=1, keepdims=True),
                      _dot_nt(pr.astype(BF16), va_buf[slot].astype(BF16))))

    def sb_block(slot, carry, accb):
        y = _dot(qb_b, kb_buf[slot].astype(BF16)) * LOG2E
        sp = _softplus2(y)
        lk = -sp
        aft = _suffix_sums(lk, upper2)
        a = jnp.exp2(y - sp + aft + carry)
        accb = accb + _dot_nt(a.astype(BF16), vb_buf[slot].astype(BF16))
        return carry + aft[:, 0:1] + lk[:, 0:1], accb

    def alive(carry):
        return (jnp.max(carry) > SB_EXIT2).astype(jnp.int32)

    carry, accb = sb_block(b % 2, carry, accb)

    def older_cond(c):
        return (c[0] < n_blk) & (c[1] > 0)

    def older_body(c):
        cps = b_copies(b, c[0], B_SLOTS - 1)
        for cp in cps:
            cp.start()
        for cp in cps:
            cp.wait()
        carry, accb = sb_block(B_SLOTS - 1, c[2], c[3])
        return c[0] + 1, alive(carry), carry, accb

    accb = lax.while_loop(older_cond, older_body, (jnp.int32(1), alive(carry), carry, accb))[3]

    sel = []
    for nb in range(n_blk):
        rank = jnp.zeros((rows, 1), jnp.int32)
        for mth in range(n_blk):
            if mth < nb:
                rank += jnp.where(gates[mth] >= gates[nb], 1, 0)
            elif mth > nb:
                rank += jnp.where(gates[mth] > gates[nb], 1, 0)
        sel.append(rank < MOBA_TOPK)
    big = parts[0][0]
    for nb in range(n_blk):
        big = jnp.maximum(big, jnp.where(sel[nb], parts[nb + 1][0], -jnp.inf))
    w = jnp.exp(parts[0][0] - big)
    l = w * parts[0][1]
    acc = w * parts[0][2]
    for nb in range(n_blk):
        m_nb, l_nb, acc_nb = parts[nb + 1]
        w = jnp.where(sel[nb], jnp.exp(jnp.minimum(m_nb - big, 0.0)), 0.0)
        l = l + w * l_nb
        acc = acc + w * acc_nb
    _reduce_heads(acc / l, oa_ref, b)
    _reduce_heads(accb, ob_ref, b)


def _sample_attn(zs3, caches, layer, page_table, slopes):
    t_new, bsz, _ = zs3.shape
    n_pages = page_table.shape[1]
    n_blk = n_pages // PAGES_PER_BLK
    assert n_pages % PAGES_PER_BLK == 0
    hbm = pl.BlockSpec(memory_space=pl.ANY)
    blk_buf = lambda n: pltpu.VMEM((n, W_ATT, MOBA_BLOCK), F32)
    out_spec = pl.BlockSpec((t_new, bsz, W_ATT), lambda b, pt: (0, 0, 0))
    grid_spec = pltpu.PrefetchScalarGridSpec(
        num_scalar_prefetch=1, grid=(bsz,),
        in_specs=[pl.BlockSpec((t_new, bsz, IN_WIDTH), lambda b, pt: (0, 0, 0)),
                  hbm, hbm, hbm, hbm],
        out_specs=[out_spec, out_spec],
        scratch_shapes=[blk_buf(2 * n_blk), blk_buf(2 * n_blk),
                        blk_buf(B_SLOTS), blk_buf(B_SLOTS),
                        pltpu.SemaphoreType.DMA((2 * n_blk, 2 * PAGES_PER_BLK)),
                        pltpu.SemaphoreType.DMA((B_SLOTS, 2 * PAGES_PER_BLK))])
    body = functools.partial(_sample_attn_body, layer=layer, bsz=bsz, n_pages=n_pages,
                             t_new=t_new, slopes=tuple(float(s) for s in slopes))
    return pl.pallas_call(
        body, grid_spec=grid_spec,
        out_shape=[jax.ShapeDtypeStruct((t_new, bsz, W_ATT), F32)] * 2,
        compiler_params=_params(("arbitrary",)), name="sample_attn")(
            page_table.reshape(-1), zs3, *caches)


def _gate_matrix(kmean_t):
    rows = jnp.arange(W_ATT)[:, None] // HEAD_DIM
    cols = jnp.arange(LANES)[None, :] // 16
    tiled = jnp.tile(kmean_t[:, :, :16], (1, 1, LANES // 16))
    return jnp.swapaxes(jnp.where(rows == cols, tiled, 0.0), 1, 2)


def kernel(x_prompt, x_sample, cache_k_a, cache_v_a, cache_k_b, cache_v_b, page_table, ln_in_g, ln_in_b, w_in, w_out, w_spatial, b_spatial, ln_c_g, ln_c_b, ln1_g, ln1_b, w_ff1, w_ff2, ln2_g, ln2_b):
    batch, seq, _ = x_prompt.shape
    dec_batch, dec_seq, _ = x_sample.shape
    depth = w_in.shape[0]
    alpha = (2 * depth) ** 0.25
    slopes_np = _alibi_slopes(N_HEADS)
    slopes = jnp.asarray(slopes_np)
    n_blk = seq // MOBA_BLOCK
    assert n_blk <= 16 and seq % 512 == 0 and dec_seq <= CHUNK and dec_batch % 8 == 0
    assert all(math.frexp(float(s))[0] == 0.5 for s in slopes_np)

    w_in_b = w_in.astype(BF16)
    w_q = jnp.concatenate([w_in_b[:, :, OFF_QA:OFF_QA + W_ATT], w_in_b[:, :, OFF_QB:OFF_QB + W_ATT],
                           w_in_b[:, :, OFF_UC:]], axis=2)
    w_kv_t = jnp.swapaxes(jnp.concatenate(
        [w_in_b[:, :, OFF_KA:OFF_KA + 2 * W_ATT], w_in_b[:, :, OFF_KB:OFF_KB + 2 * W_ATT]],
        axis=2), 1, 2)
    w_out_b = w_out.astype(BF16)
    w_ff1_b = w_ff1.astype(BF16)
    w_ff2_b = w_ff2.astype(BF16)
    caches = [jnp.transpose(c, (0, 1, 3, 4, 2)).reshape(c.shape[0], c.shape[1], W_ATT, PAGE)
              for c in (cache_k_a, cache_v_a, cache_k_b, cache_v_b)]

    bias_p = jnp.repeat(jnp.swapaxes(b_spatial, 1, 2), HEAD_DIM, axis=2)
    coef_s = jnp.repeat(jnp.transpose(w_spatial[:, :, :dec_seq, :dec_seq], (0, 2, 3, 1)),
                        HEAD_DIM, axis=3).reshape(depth, dec_seq * dec_seq, W_C)
    bias_s = jnp.repeat(jnp.swapaxes(b_spatial[:, :, :dec_seq], 1, 2), HEAD_DIM, axis=2)

    hp = x_prompt.reshape(batch * seq, D_MODEL)
    hs = jnp.swapaxes(x_sample, 0, 1).reshape(dec_seq * dec_batch, D_MODEL)
    kv_p = [jnp.zeros((depth, batch, W_ATT, seq), F32) for _ in range(4)]
    kv_s = [jnp.zeros((depth, dec_seq, W_ATT, dec_batch), F32) for _ in range(4)]
    vc_s = []
    for l in range(depth):
        ln = (ln_in_g, ln_in_b) if l == 0 else None
        lnc_g, lnc_b = ln_c_g[l].reshape(-1), ln_c_b[l].reshape(-1)

        hp, zb, kv_p, kmean_t = _in_proj_prompt(hp, w_q, w_kv_t, ln, kv_p, l, depth, batch, seq)
        selb = _moba_gate(zb, _gate_matrix(kmean_t), batch, seq)
        mix_a = _moba_prompt(zb, kv_p[0], kv_p[1], selb, slopes, l, batch, seq)
        mix_b = _sb_prompt(zb, kv_p[2], kv_p[3], l, batch, seq)
        mix_c = _gmlp_prompt(zb, w_spatial[l], bias_p[l], lnc_g, lnc_b)
        h1 = _out_proj(mix_a, mix_b, mix_c, hp, w_out_b, l, ln1_g[l], ln1_b[l], alpha)
        hp = _ffn(h1, w_ff1_b, w_ff2_b, l, ln2_g[l], ln2_b[l], alpha)

        hs, zs, kv_s = _in_proj_sample(hs, w_in_b, w_kv_t, ln, kv_s, l, depth, dec_seq, dec_batch)
        oa, ob = _sample_attn(zs.reshape(dec_seq, dec_batch, IN_WIDTH), caches, l,
                              page_table, slopes_np)
        mix_c, vn = _gmlp_sample(zs, coef_s[l], bias_s[l], lnc_g, lnc_b, dec_seq, dec_batch)
        vc_s.append(vn)
        h1 = _out_proj(oa.reshape(-1, W_ATT), ob.reshape(-1, W_ATT), mix_c, hs,
                       w_out_b, l, ln1_g[l], ln1_b[l], alpha)
        hs = _ffn(h1, w_ff1_b, w_ff2_b, l, ln2_g[l], ln2_b[l], alpha)

    kv_out_p = [jnp.transpose(a.reshape(depth, batch, N_HEADS, HEAD_DIM, seq), (0, 1, 4, 2, 3))
                for a in kv_p]
    kv_out_s = [jnp.transpose(a.reshape(depth, dec_seq, N_HEADS, HEAD_DIM, dec_batch), (0, 4, 1, 2, 3))
                for a in kv_s]
    vc_out = jnp.swapaxes(jnp.stack(vc_s).reshape(depth, dec_seq, dec_batch, W_C), 1, 2)
    y_s = jnp.swapaxes(hs.reshape(dec_seq, dec_batch, D_MODEL), 0, 1)
    return (hp.reshape(batch, seq, D_MODEL), y_s, *kv_out_p, *kv_out_s, vc_out)
```

```python
import functools
import math

import jax
import jax.numpy as jnp
import numpy as np
from jax import lax
from jax.experimental import pallas as pl
from jax.experimental.pallas import tpu as pltpu

F32 = jnp.float32
BF16 = jnp.bfloat16

D_MODEL = 1024
HEAD_DIM = 64
N_HEADS = 6
W_ATT = N_HEADS * HEAD_DIM
C_GROUPS = 4
W_C = C_GROUPS * HEAD_DIM
IN_WIDTH = 6 * W_ATT + 2 * W_C
D_FF = 4 * D_MODEL
MOBA_BLOCK = 256
MOBA_TOPK = 3
CHUNK = 128
PAGE = 128
LN_EPS = 1e-5
LANES = 128
NEG_BIG = -1e30
SB_EXIT = -110.0

OFF_QA, OFF_KA, OFF_VA = 0, W_ATT, 2 * W_ATT
OFF_QB, OFF_KB, OFF_VB = 3 * W_ATT, 4 * W_ATT, 5 * W_ATT
OFF_UC, OFF_VC = 6 * W_ATT, 6 * W_ATT + W_C
ZP_QA, ZP_QB, ZP_UC, ZP_VC = 0, W_ATT, 2 * W_ATT, 2 * W_ATT + W_C
ZP_WIDTH = 2 * W_ATT + 2 * W_C

VMEM_LIMIT = 56 * 1024 * 1024


def _alibi_slopes(n):
    def pow2(m):
        start = 2.0 ** (-8.0 / m)
        return [start ** (i + 1) for i in range(m)]
    p = 2 ** int(math.floor(math.log2(n)))
    s = pow2(p)
    if p < n:
        s = s + pow2(2 * p)[0::2][: n - p]
    return np.array(s, dtype=np.float32)


def _ln(x, g, b):
    mu = jnp.mean(x, axis=-1, keepdims=True)
    d = x - mu
    var = jnp.mean(d * d, axis=-1, keepdims=True)
    return d * lax.rsqrt(var + LN_EPS) * g + b


def _dot(a, b, precision=None):
    return jnp.dot(a, b, preferred_element_type=F32, precision=precision)


def _dot_nt(a, b, precision=None):
    return lax.dot_general(a, b, (((1,), (1,)), ((), ())),
                           preferred_element_type=F32, precision=precision)


def _params(sem):
    return pltpu.CompilerParams(dimension_semantics=sem, vmem_limit_bytes=VMEM_LIMIT)


def _in_proj_prompt_body(*refs, apply_ln, tm):
    it = iter(refs)
    x_ref = next(it)
    if apply_ln:
        g_ref, b_ref = next(it), next(it)
    wq_ref, wkv_ref = next(it), next(it)
    for _ in range(4):
        next(it)
    if apply_ln:
        h_ref = next(it)
    zb_ref = next(it)
    kv_refs = [next(it) for _ in range(4)]
    km_ref = next(it)
    ti = pl.program_id(1)

    x = x_ref[...]
    if apply_ln:
        x = _ln(x, g_ref[...], b_ref[...])
        h_ref[...] = x
    hb = x.astype(BF16)
    for c in range(0, ZP_WIDTH, 256):
        zb_ref[:, c:c + 256] = _dot(hb, wq_ref[:, c:c + 256]).astype(BF16)
    lane = lax.broadcasted_iota(jnp.int32, (W_ATT, LANES), 1)

    @pl.when(ti == 0)
    def _zero_means():
        km_ref[...] = jnp.zeros_like(km_ref)

    kt_all = _dot_nt(wkv_ref[...], hb)
    for t in range(4):
        kt = kt_all[t * W_ATT:(t + 1) * W_ATT]
        kv_refs[t][...] = kt
        if t == 0:
            km = km_ref[...]
            for r in range(tm // MOBA_BLOCK):
                mean = jnp.sum(kt[:, r * MOBA_BLOCK:(r + 1) * MOBA_BLOCK], axis=1,
                               keepdims=True) * (1.0 / MOBA_BLOCK)
                km = jnp.where(lane == ti * (tm // MOBA_BLOCK) + r, mean, km)
            km_ref[...] = km


def _in_proj_prompt(x, wq, wkv, ln, kv_prev, layer, depth, batch, seq):
    n = x.shape[0]
    tm = 512
    nt = seq // tm
    row = lambda b, t: (b * nt + t, 0)
    const = lambda b, t: (0, 0)
    args, in_specs = [x], [pl.BlockSpec((tm, D_MODEL), row)]
    if ln is not None:
        args += [ln[0].reshape(1, D_MODEL), ln[1].reshape(1, D_MODEL)]
        in_specs += [pl.BlockSpec((1, D_MODEL), const)] * 2
    args += [wq, wkv]
    in_specs += [pl.BlockSpec((None, D_MODEL, ZP_WIDTH), lambda b, t: (layer, 0, 0)),
                 pl.BlockSpec((None, 4 * W_ATT, D_MODEL), lambda b, t: (layer, 0, 0))]
    out_shape, out_specs = [], []
    if ln is not None:
        out_shape.append(jax.ShapeDtypeStruct((n, D_MODEL), F32))
        out_specs.append(pl.BlockSpec((tm, D_MODEL), row))
    out_shape.append(jax.ShapeDtypeStruct((n, ZP_WIDTH), BF16))
    out_specs.append(pl.BlockSpec((tm, ZP_WIDTH), row))
    aliases = {}
    for t in range(4):
        aliases[len(args)] = len(out_shape)
        args.append(kv_prev[t])
        in_specs.append(pl.BlockSpec(memory_space=pl.ANY))
        out_shape.append(jax.ShapeDtypeStruct((depth, batch, W_ATT, seq), F32))
        out_specs.append(pl.BlockSpec((None, None, W_ATT, tm), lambda b, t: (layer, b, 0, t)))
    out_shape.append(jax.ShapeDtypeStruct((batch, W_ATT, LANES), F32))
    out_specs.append(pl.BlockSpec((None, W_ATT, LANES), lambda b, t: (b, 0, 0)))
    body = functools.partial(_in_proj_prompt_body, apply_ln=ln is not None, tm=tm)
    outs = list(pl.pallas_call(
        body, grid=(batch, nt), in_specs=in_specs, out_specs=out_specs, out_shape=out_shape,
        input_output_aliases=aliases, compiler_params=_params(("parallel", "arbitrary")),
        name="in_proj_prompt")(*args))
    h = outs.pop(0) if ln is not None else x
    return h, outs[0], outs[1:5], outs[5]


def _in_proj_sample_body(*refs, apply_ln, t_new, bsz):
    it = iter(refs)
    x_ref = next(it)
    if apply_ln:
        g_ref, b_ref = next(it), next(it)
    w_ref, wkv_ref = next(it), next(it)
    for _ in range(4):
        next(it)
    if apply_ln:
        h_ref = next(it)
    zs_ref = next(it)
    kv_refs = [next(it) for _ in range(4)]

    x = x_ref[...]
    if apply_ln:
        x = _ln(x, g_ref[...], b_ref[...])
        h_ref[...] = x
    hb = x.astype(BF16)
    for c in range(0, IN_WIDTH, 256):
        zs_ref[:, c:c + 256] = _dot(hb, w_ref[:, c:c + 256])
    for t in range(4):
        kt = _dot_nt(wkv_ref[t * W_ATT:(t + 1) * W_ATT, :], hb)
        for tt in range(t_new):
            kv_refs[t][tt] = kt[:, tt * bsz:(tt + 1) * bsz]


def _in_proj_sample(x, w, wkv, ln, kv_prev, layer, depth, t_new, bsz):
    n = x.shape[0]
    const = lambda i: (0, 0)
    args, in_specs = [x], [pl.BlockSpec((n, D_MODEL), const)]
    if ln is not None:
        args += [ln[0].reshape(1, D_MODEL), ln[1].reshape(1, D_MODEL)]
        in_specs += [pl.BlockSpec((1, D_MODEL), const)] * 2
    args += [w, wkv]
    in_specs += [pl.BlockSpec((None, D_MODEL, IN_WIDTH), lambda i: (layer, 0, 0)),
                 pl.BlockSpec((None, 4 * W_ATT, D_MODEL), lambda i: (layer, 0, 0))]
    out_shape, out_specs = [], []
    if ln is not None:
        out_shape.append(jax.ShapeDtypeStruct((n, D_MODEL), F32))
        out_specs.append(pl.BlockSpec((n, D_MODEL), const))
    out_shape.append(jax.ShapeDtypeStruct((n, IN_WIDTH), F32))
    out_specs.append(pl.BlockSpec((n, IN_WIDTH), const))
    aliases = {}
    for t in range(4):
        aliases[len(args)] = len(out_shape)
        args.append(kv_prev[t])
        in_specs.append(pl.BlockSpec(memory_space=pl.ANY))
        out_shape.append(jax.ShapeDtypeStruct((depth, t_new, W_ATT, bsz), F32))
        out_specs.append(pl.BlockSpec((None, t_new, W_ATT, bsz), lambda i: (layer, 0, 0, 0)))
    body = functools.partial(_in_proj_sample_body, apply_ln=ln is not None, t_new=t_new, bsz=bsz)
    outs = list(pl.pallas_call(
        body, grid=(1,), in_specs=in_specs, out_specs=out_specs, out_shape=out_shape,
        input_output_aliases=aliases, compiler_params=_params(("arbitrary",)),
        name="in_proj_sample")(*args))
    h = outs.pop(0) if ln is not None else x
    return h, outs[0], outs[1:5]


def _gate_body(q_ref, kmt_ref, o_ref):
    cur = pl.program_id(1)
    g = _dot_nt(kmt_ref[...], q_ref[...].astype(F32), precision=lax.Precision.HIGHEST)
    blk = lax.broadcasted_iota(jnp.int32, g.shape, 0) & 15
    past = blk < cur
    g = jnp.where(past, g, -jnp.inf)
    rank = jnp.zeros(g.shape, jnp.int32)
    for s in range(1, 16):
        lower = pltpu.roll(g, s, axis=0)
        rank += jnp.where(lower >= g, jnp.where(blk >= s, 1, 0), 0)
        upper = pltpu.roll(g, LANES - s, axis=0)
        rank += jnp.where(upper > g, jnp.where(blk + s <= 15, 1, 0), 0)
    sel = jnp.where(past, jnp.where(rank < MOBA_TOPK, 0.0, NEG_BIG), NEG_BIG)
    o_ref[...] = sel.T


def _moba_gate(zb, kmt, batch, seq):
    nq = seq // MOBA_BLOCK
    return pl.pallas_call(
        _gate_body, grid=(batch, nq),
        in_specs=[pl.BlockSpec((MOBA_BLOCK, W_ATT), lambda b, i: (b * nq + i, ZP_QA // W_ATT)),
                  pl.BlockSpec((None, LANES, W_ATT), lambda b, i: (b, 0, 0))],
        out_specs=pl.BlockSpec((MOBA_BLOCK, LANES), lambda b, i: (b * nq + i, 0)),
        out_shape=jax.ShapeDtypeStruct((batch * seq, LANES), F32),
        compiler_params=_params(("parallel", "parallel")), name="moba_gate")(zb, kmt)


ROLE_SEL0, ROLE_SEL1 = 0, 1
ROLE_ROW, ROLE_KEY = 96, 97
ROLE_BLK, ROLE_2ND = 98, 99


ATT_PAIRS = 3


def _moba_body(slopes_ref, q_ref, kt_ref, vt_ref, selb_ref, o_ref, lhs_s, rhs_s):
    pg = pl.program_id(1)
    i = pl.program_id(2)
    tq = MOBA_BLOCK
    selb = selb_ref[...]
    lane = lax.broadcasted_iota(jnp.int32, (2 * tq, LANES), 1)
    row2 = lax.broadcasted_iota(jnp.int32, (2 * tq, LANES), 0)
    head1 = row2 >= tq
    rowf = jnp.where(head1, row2 - tq, row2).astype(F32)
    role = lax.broadcasted_iota(jnp.int32, (LANES, 2 * tq), 0)
    key2 = lax.broadcasted_iota(jnp.int32, (LANES, 2 * tq), 1)
    second = key2 >= tq
    keyf = jnp.where(second, key2 - tq, key2).astype(F32)
    rhs_roles = jnp.where(
        (role == ROLE_SEL0) & jnp.logical_not(second), 1.0,
        jnp.where(((role == ROLE_SEL1) | (role == ROLE_2ND)) & second, 1.0,
                  jnp.where((role == ROLE_ROW) | (role == ROLE_BLK), 1.0,
                            jnp.where(role == ROLE_KEY, keyf, 0.0)))).astype(BF16)
    rq = jnp.where(lax.broadcasted_iota(jnp.int32, (2 * tq, tq), 0) >= tq,
                   lax.broadcasted_iota(jnp.int32, (2 * tq, tq), 0) - tq,
                   lax.broadcasted_iota(jnp.int32, (2 * tq, tq), 0))
    ck = lax.broadcasted_iota(jnp.int32, (2 * tq, tq), 1)
    d0 = (rq - ck).astype(F32)
    causal = ck <= rq
    row0 = pl.multiple_of(i * tq, tq)
    fixed = (lane == ROLE_ROW) | (lane == ROLE_KEY) | (lane == ROLE_2ND)

    pairs = []
    for pr_i in range(ATT_PAIRS):
        lanes = slice(pr_i * LANES, (pr_i + 1) * LANES)
        rhs_s[pr_i, LANES:2 * LANES, :] = rhs_roles
        q = q_ref[:, lanes].astype(F32) * (HEAD_DIM ** -0.5)
        q2 = jnp.concatenate([q, q], axis=0)
        qh = jnp.where((lane >= HEAD_DIM) == head1, q2, 0.0).astype(BF16)
        lhs_s[pr_i, :, 0:LANES] = qh
        hd0 = 2 * (pg * ATT_PAIRS + pr_i)
        slope = jnp.where(lax.broadcasted_iota(jnp.int32, (2 * tq, 1), 0) >= tq,
                          slopes_ref[hd0 + 1], slopes_ref[hd0])
        lhs_fix = jnp.where(lane == ROLE_ROW, -slope * rowf,
                            jnp.where(lane == ROLE_KEY, slope, slope * tq))
        pairs.append((hd0, slope, lhs_fix, qh))

    def own_block():
        state = []
        for pr_i in range(ATT_PAIRS):
            lanes = slice(pr_i * LANES, (pr_i + 1) * LANES)
            _, slope, _, qh = pairs[pr_i]
            kd = kt_ref[lanes, pl.ds(row0, tq)].astype(BF16)
            vd = vt_ref[lanes, pl.ds(row0, tq)].astype(BF16)
            s = jnp.where(causal, _dot(qh, kd) - slope * d0, -jnp.inf)
            m = jnp.max(s, axis=1, keepdims=True)
            pr = jnp.exp(s - m)
            state += [m, jnp.sum(pr, axis=1, keepdims=True), _dot_nt(pr.astype(BF16), vd)]
        return tuple(state)

    def body(pp, carry):
        j0 = 2 * pp
        r0 = pl.multiple_of(j0 * tq, 2 * tq)
        out = []
        for pr_i in range(ATT_PAIRS):
            lanes = slice(pr_i * LANES, (pr_i + 1) * LANES)
            hd0, slope, lhs_fix, _ = pairs[pr_i]
            m, l, acc = carry[3 * pr_i:3 * pr_i + 3]
            rhs_s[pr_i, 0:LANES, :] = kt_ref[lanes, pl.ds(r0, 2 * tq)].astype(BF16)
            v2 = vt_ref[lanes, pl.ds(r0, 2 * tq)].astype(BF16)
            sel = jnp.concatenate(
                [pltpu.roll(selb, (LANES - (hd0 + hh) * 16 - j0) % LANES, axis=1) for hh in range(2)],
                axis=0)
            off = -slope * ((i - j0) * tq).astype(F32)
            ext = jnp.where(fixed, lhs_fix, jnp.where(lane == ROLE_BLK, off, sel))
            lhs_s[pr_i, :, LANES:2 * LANES] = ext.astype(BF16)
            t = _dot(lhs_s[pr_i], rhs_s[pr_i])
            m_new = jnp.maximum(m, jnp.max(t, axis=1, keepdims=True))
            alpha = jnp.exp(m - m_new)
            pr = jnp.exp(t - m_new)
            l = alpha * l + jnp.sum(pr, axis=1, keepdims=True)
            acc = alpha * acc + _dot_nt(pr.astype(BF16), v2)
            out += [m_new, l, acc]
        return tuple(out)

    state = lax.fori_loop(0, (i + 1) // 2, body, own_block())
    for pr_i in range(ATT_PAIRS):
        o = state[3 * pr_i + 2] / state[3 * pr_i + 1]
        o_ref[:, pr_i * LANES:(pr_i + 1) * LANES] = jnp.where(
            lax.broadcasted_iota(jnp.int32, (tq, LANES), 1) < HEAD_DIM, o[:tq], o[tq:]).astype(o_ref.dtype)


def _moba_prompt(zb, kt, vt, selb, slopes, layer, batch, seq):
    nq = seq // MOBA_BLOCK
    wq = ATT_PAIRS * LANES
    cq = ZP_QA // wq
    kv_spec = pl.BlockSpec((None, None, wq, seq), lambda b, p, i, s: (layer, b, p, 0))
    grid_spec = pltpu.PrefetchScalarGridSpec(
        num_scalar_prefetch=1, grid=(batch, N_HEADS // 2 // ATT_PAIRS, nq),
        in_specs=[pl.BlockSpec((MOBA_BLOCK, wq), lambda b, p, i, s: (b * nq + i, cq + p)),
                  kv_spec, kv_spec,
                  pl.BlockSpec((MOBA_BLOCK, LANES), lambda b, p, i, s: (b * nq + i, 0))],
        out_specs=pl.BlockSpec((MOBA_BLOCK, wq), lambda b, p, i, s: (b * nq + i, p)),
        scratch_shapes=[pltpu.VMEM((ATT_PAIRS, 2 * MOBA_BLOCK, 2 * LANES), BF16),
                        pltpu.VMEM((ATT_PAIRS, 2 * LANES, 2 * MOBA_BLOCK), BF16)])
    return pl.pallas_call(
        _moba_body, grid_spec=grid_spec,
        out_shape=jax.ShapeDtypeStruct((batch * seq, W_ATT), BF16),
        compiler_params=_params(("parallel", "parallel", "arbitrary")),
        name="moba_prompt")(slopes, zb, kt, vt, selb)


LOG2E = 1.4426950408889634
SB_EXIT2 = SB_EXIT * LOG2E


def _softplus2(y):
    return jnp.maximum(y, 0.0) + jnp.log2(1.0 + jnp.exp2(-jnp.abs(y)))


def _suffix_sums(lk, upper2):
    hi = lk.astype(BF16)
    lo = (lk - hi.astype(F32)).astype(BF16)
    return _dot(jnp.concatenate([hi, lo], axis=1), upper2)


def _upper2(n):
    r = lax.broadcasted_iota(jnp.int32, (2 * n, n), 0)
    c = lax.broadcasted_iota(jnp.int32, (2 * n, n), 1)
    return jnp.where(jnp.where(r >= n, r - n, r) > c, 1.0, 0.0).astype(BF16)


def _sb_body(q_ref, kt_ref, vt_ref, o_ref):
    i = pl.program_id(2)
    tq = MOBA_BLOCK
    lane = lax.broadcasted_iota(jnp.int32, (2 * tq, LANES), 1)
    row2 = lax.broadcasted_iota(jnp.int32, (2 * tq, tq), 0)
    rq = jnp.where(row2 >= tq, row2 - tq, row2)
    ck = lax.broadcasted_iota(jnp.int32, (2 * tq, tq), 1)
    before = ck < rq
    upper2 = _upper2(tq)
    head1 = lax.broadcasted_iota(jnp.int32, (2 * tq, LANES), 0) >= tq
    qhs = []
    for pr_i in range(ATT_PAIRS):
        q = q_ref[:, pr_i * LANES:(pr_i + 1) * LANES].astype(F32) * (HEAD_DIM ** -0.5)
        q2 = jnp.concatenate([q, q], axis=0)
        qhs.append(jnp.where((lane >= HEAD_DIM) == head1, q2, 0.0).astype(BF16))

    def block(j, st, diag):
        r0 = pl.multiple_of(j * tq, tq)
        ys, sps, lks, vbs = [], [], [], []
        for pr_i in range(ATT_PAIRS):
            lanes = slice(pr_i * LANES, (pr_i + 1) * LANES)
            kb = kt_ref[lanes, pl.ds(r0, tq)].astype(BF16)
            vbs.append(vt_ref[lanes, pl.ds(r0, tq)].astype(BF16))
            y = _dot(qhs[pr_i], kb) * LOG2E
            sp = _softplus2(y)
            lk = -sp
            if diag:
                lk = jnp.where(before, lk, 0.0)
            ys.append(y)
            sps.append(sp)
            lks.append(lk)
        aft_all = _suffix_sums(jnp.concatenate(lks, axis=0), upper2)
        out = []
        for pr_i in range(ATT_PAIRS):
            carry, acc = st[2 * pr_i], st[2 * pr_i + 1]
            aft = aft_all[pr_i * 2 * tq:(pr_i + 1) * 2 * tq]
            a = jnp.exp2(ys[pr_i] - sps[pr_i] + aft + carry)
            if diag:
                a = jnp.where(before, a, 0.0)
            out += [carry + aft[:, 0:1] + lks[pr_i][:, 0:1],
                    acc + _dot_nt(a.astype(BF16), vbs[pr_i])]
        return tuple(out)

    def alive(st):
        top = st[0]
        for pr_i in range(1, ATT_PAIRS):
            top = jnp.maximum(top, st[2 * pr_i])
        return (jnp.max(top) > SB_EXIT2).astype(jnp.int32)

    zc, za = jnp.zeros((2 * tq, 1), F32), jnp.zeros((2 * tq, LANES), F32)
    init = (zc, za) * ATT_PAIRS
    st = lax.cond(i >= 1,
                  lambda: block(i - 1, block(i, init, True), False),
                  lambda: block(i, init, True))

    def cond(c):
        return (c[0] >= 0) & (c[1] > 0)

    def body(c):
        st = block(c[0], c[2:], False)
        return (c[0] - 1, alive(st)) + st

    res = lax.while_loop(cond, body, (i - 2, alive(st)) + st)
    for pr_i in range(ATT_PAIRS):
        acc = res[2 + 2 * pr_i + 1]
        o_ref[:, pr_i * LANES:(pr_i + 1) * LANES] = jnp.where(
            lax.broadcasted_iota(jnp.int32, (tq, LANES), 1) < HEAD_DIM, acc[:tq], acc[tq:]).astype(o_ref.dtype)


def _sb_prompt(zb, kt, vt, layer, batch, seq):
    nq = seq // MOBA_BLOCK
    wq = ATT_PAIRS * LANES
    cq = ZP_QB // wq
    kv_spec = pl.BlockSpec((None, None, wq, seq), lambda b, p, i: (layer, b, p, 0))
    return pl.pallas_call(
        _sb_body, grid=(batch, N_HEADS // 2 // ATT_PAIRS, nq),
        in_specs=[pl.BlockSpec((MOBA_BLOCK, wq), lambda b, p, i: (b * nq + i, cq + p)),
                  kv_spec, kv_spec],
        out_specs=pl.BlockSpec((MOBA_BLOCK, wq), lambda b, p, i: (b * nq + i, p)),
        out_shape=jax.ShapeDtypeStruct((batch * seq, W_ATT), BF16),
        compiler_params=_params(("parallel", "parallel", "arbitrary")),
        name="sb_prompt")(zb, kt, vt)


def _group_norm(v, g, b):
    gi = (lax.broadcasted_iota(jnp.int32, (3 * W_C, W_C), 0) % W_C) // HEAD_DIM
    gj = lax.broadcasted_iota(jnp.int32, (3 * W_C, W_C), 1) // HEAD_DIM
    gmean3 = jnp.where(gi == gj, 1.0 / HEAD_DIM, 0.0).astype(BF16)

    def group_mean(x):
        hi = x.astype(BF16)
        r = x - hi.astype(F32)
        mid = r.astype(BF16)
        lo = (r - mid.astype(F32)).astype(BF16)
        return _dot(jnp.concatenate([hi, mid, lo], axis=1), gmean3)

    d = v - group_mean(v)
    var = group_mean(d * d)
    return d * lax.rsqrt(var + LN_EPS) * g + b


def _gmlp_prompt_body(u_ref, v_ref, ws_ref, bias_ref, g_ref, b_ref, o_ref, *, rows):
    ti = lax.broadcasted_iota(jnp.int32, (CHUNK, CHUNK), 0)
    tj = lax.broadcasted_iota(jnp.int32, (CHUNK, CHUNK), 1)
    lane = lax.broadcasted_iota(jnp.int32, (CHUNK, LANES), 1)
    wpair = []
    for pr in range(2):
        w0 = jnp.where(tj <= ti, ws_ref[2 * pr], 0.0)
        w1 = jnp.where(tj <= ti, ws_ref[2 * pr + 1], 0.0)
        wpair.append(jnp.concatenate([w0, w1], axis=1).astype(BF16))
    vn_all = _group_norm(jax.nn.gelu(v_ref[...].astype(F32)), g_ref[...], b_ref[...])
    for r in range(rows // CHUNK):
        sl = slice(r * CHUNK, (r + 1) * CHUNK)
        u = jax.nn.gelu(u_ref[sl, :].astype(F32))
        vn = vn_all[sl, :]
        mixes = []
        for pr in range(2):
            vp = vn[:, pr * LANES:(pr + 1) * LANES]
            rhs = jnp.concatenate([jnp.where(lane < HEAD_DIM, vp, 0.0),
                                   jnp.where(lane >= HEAD_DIM, vp, 0.0)], axis=0).astype(BF16)
            mixes.append(_dot(wpair[pr], rhs))
        mix = jnp.concatenate(mixes, axis=1) + bias_ref[...]
        o_ref[sl, :] = (u * mix).astype(o_ref.dtype)


def _gmlp_prompt(zb, ws, bias, g, b):
    n = zb.shape[0]
    rows = 512
    return pl.pallas_call(
        functools.partial(_gmlp_prompt_body, rows=rows), grid=(n // rows,),
        in_specs=[pl.BlockSpec((rows, W_C), lambda i: (i, ZP_UC // W_C)),
                  pl.BlockSpec((rows, W_C), lambda i: (i, ZP_VC // W_C)),
                  pl.BlockSpec((C_GROUPS, CHUNK, CHUNK), lambda i: (0, 0, 0)),
                  pl.BlockSpec((CHUNK, W_C), lambda i: (0, 0)),
                  pl.BlockSpec((1, W_C), lambda i: (0, 0)),
                  pl.BlockSpec((1, W_C), lambda i: (0, 0))],
        out_specs=pl.BlockSpec((rows, W_C), lambda i: (i, 0)),
        out_shape=jax.ShapeDtypeStruct((n, W_C), BF16),
        compiler_params=_params(("parallel",)), name="gmlp_prompt")(
            zb, zb, ws, bias, g.reshape(1, W_C), b.reshape(1, W_C))


def _gmlp_sample_body(u_ref, v_ref, coef_ref, bias_ref, g_ref, b_ref, o_ref, vn_ref, *, t_new, bsz):
    vns = []
    for t in range(t_new):
        sl = slice(t * bsz, (t + 1) * bsz)
        vn = _group_norm(jax.nn.gelu(v_ref[sl, :]), g_ref[...], b_ref[...])
        vn_ref[sl, :] = vn
        vns.append(vn)
    for t in range(t_new):
        sl = slice(t * bsz, (t + 1) * bsz)
        mix = bias_ref[t:t + 1, :]
        for s in range(t + 1):
            mix = mix + coef_ref[t * t_new + s:t * t_new + s + 1, :] * vns[s]
        o_ref[sl, :] = (jax.nn.gelu(u_ref[sl, :]) * mix).astype(o_ref.dtype)


def _gmlp_sample(zs, coef, bias, g, b, t_new, bsz):
    n = zs.shape[0]
    full = lambda shape: pl.BlockSpec(shape, lambda i: (0, 0))
    return pl.pallas_call(
        functools.partial(_gmlp_sample_body, t_new=t_new, bsz=bsz), grid=(1,),
        in_specs=[pl.BlockSpec((n, W_C), lambda i: (0, OFF_UC // W_C)),
                  pl.BlockSpec((n, W_C), lambda i: (0, OFF_VC // W_C)),
                  full((t_new * t_new, W_C)), full((t_new, W_C)), full((1, W_C)), full((1, W_C))],
        out_specs=[full((n, W_C)), full((n, W_C))],
        out_shape=[jax.ShapeDtypeStruct((n, W_C), BF16), jax.ShapeDtypeStruct((n, W_C), F32)],
        compiler_params=_params(("arbitrary",)), name="gmlp_sample")(
            zs, zs, coef, bias, g.reshape(1, W_C), b.reshape(1, W_C))


def _out_proj_body(ma_ref, mb_ref, mc_ref, h_ref, w_ref, g_ref, b_ref, o_ref, *, alpha):
    mixed = jnp.concatenate([ma_ref[...].astype(BF16), mb_ref[...].astype(BF16),
                             mc_ref[...].astype(BF16)], axis=1)
    y = _dot(mixed, w_ref[...])
    o_ref[...] = _ln(alpha * h_ref[...] + y, g_ref[...], b_ref[...])


def _out_proj(ma, mb, mc, h, w, layer, g, b, alpha):
    n = h.shape[0]
    tm = min(512, n)
    row = lambda i: (i, 0)
    const = lambda i: (0, 0)
    return pl.pallas_call(
        functools.partial(_out_proj_body, alpha=alpha), grid=(n // tm,),
        in_specs=[pl.BlockSpec((tm, W_ATT), row), pl.BlockSpec((tm, W_ATT), row),
                  pl.BlockSpec((tm, W_C), row), pl.BlockSpec((tm, D_MODEL), row),
                  pl.BlockSpec((None, D_MODEL, D_MODEL), lambda i: (layer, 0, 0)),
                  pl.BlockSpec((1, D_MODEL), const), pl.BlockSpec((1, D_MODEL), const)],
        out_specs=pl.BlockSpec((tm, D_MODEL), row),
        out_shape=jax.ShapeDtypeStruct((n, D_MODEL), F32),
        compiler_params=_params(("parallel",)), name="out_proj")(
            ma, mb, mc, h, w, g.reshape(1, D_MODEL), b.reshape(1, D_MODEL))


def _ffn_body(h_ref, w1_ref, w2_ref, g_ref, b_ref, o_ref, acc_ref, *, alpha):
    h = h_ref[...]
    hb = h.astype(BF16)
    step = 512
    for c in range(0, D_FF, step):
        a = _dot(hb, w1_ref[:, c:c + step])
        a = jnp.square(jnp.maximum(a, 0.0)).astype(BF16)
        y = _dot(a, w2_ref[c:c + step, :])
        if c == 0:
            acc_ref[...] = y
        else:
            acc_ref[...] += y
    o_ref[...] = _ln(alpha * h + acc_ref[...], g_ref[...], b_ref[...])


def _ffn(h, w1, w2, layer, g, b, alpha):
    n = h.shape[0]
    tm = min(512, n)
    row = lambda i: (i, 0)
    const = lambda i: (0, 0)
    return pl.pallas_call(
        functools.partial(_ffn_body, alpha=alpha), grid=(n // tm,),
        in_specs=[pl.BlockSpec((tm, D_MODEL), row),
                  pl.BlockSpec((None, D_MODEL, D_FF), lambda i: (layer, 0, 0),
                               pipeline_mode=pl.Buffered(1)),
                  pl.BlockSpec((None, D_FF, D_MODEL), lambda i: (layer, 0, 0),
                               pipeline_mode=pl.Buffered(1)),
                  pl.BlockSpec((1, D_MODEL), const), pl.BlockSpec((1, D_MODEL), const)],
        out_specs=pl.BlockSpec((tm, D_MODEL), row),
        out_shape=jax.ShapeDtypeStruct((n, D_MODEL), F32),
        scratch_shapes=[pltpu.VMEM((tm, D_MODEL), F32)],
        compiler_params=_params(("parallel",)), name="ffn")(
            h, w1, w2, g.reshape(1, D_MODEL), b.reshape(1, D_MODEL))


PAGES_PER_BLK = MOBA_BLOCK // PAGE
B_SLOTS = 3


def _expand_heads(q):
    r = lax.broadcasted_iota(jnp.int32, (8, W_ATT), 0)
    c = lax.broadcasted_iota(jnp.int32, (8, W_ATT), 1) // HEAD_DIM
    pieces = [jnp.where(r == c, jnp.broadcast_to(row, (8, W_ATT)), 0.0) for row in q]
    return jnp.concatenate(pieces, axis=0)


def _reduce_heads(o, out_ref, b):
    r = lax.broadcasted_iota(jnp.int32, (8, W_ATT), 0)
    c = lax.broadcasted_iota(jnp.int32, (8, W_ATT), 1) // HEAD_DIM
    for t in range(o.shape[0] // 8):
        piece = jnp.where(r == c, o[8 * t:8 * t + 8, :], 0.0)
        out_ref[t, pl.ds(b, 1), :] = jnp.sum(piece, axis=0, keepdims=True)


def _sample_attn_body(pt_ref, zs_ref, cka, cva, ckb, cvb, oa_ref, ob_ref,
                      ka_buf, va_buf, kb_buf, vb_buf, sem_a, sem_b,
                      *, layer, bsz, n_pages, t_new, slopes):
    b = pl.program_id(0)
    past_len = n_pages * PAGE
    n_blk = n_pages // PAGES_PER_BLK

    def blk_copies(src_k, src_v, dst_k, dst_v, sem, first_page, slot):
        cps = []
        for k in range(PAGES_PER_BLK):
            pg = pt_ref[first_page + k]
            win = pl.ds(k * PAGE, PAGE)
            cps.append(pltpu.make_async_copy(src_k.at[layer, pg], dst_k.at[slot, :, win],
                                             sem.at[slot, 2 * k]))
            cps.append(pltpu.make_async_copy(src_v.at[layer, pg], dst_v.at[slot, :, win],
                                             sem.at[slot, 2 * k + 1]))
        return cps

    def a_copies(seq):
        cps = []
        for k in range(n_blk):
            cps += blk_copies(cka, cva, ka_buf, va_buf, sem_a,
                              seq * n_pages + k * PAGES_PER_BLK, (seq % 2) * n_blk + k)
        return cps

    def b_copies(seq, back, slot):
        return blk_copies(ckb, cvb, kb_buf, vb_buf, sem_b,
                          seq * n_pages + (n_blk - 1 - back) * PAGES_PER_BLK, slot)

    @pl.when(b == 0)
    def _prime():
        for cp in a_copies(0) + b_copies(0, 0, 0):
            cp.start()

    for cp in a_copies(b) + b_copies(b, 0, b % 2):
        cp.wait()

    @pl.when(b + 1 < bsz)
    def _prefetch_next():
        for cp in a_copies(b + 1) + b_copies(b + 1, 0, (b + 1) % 2):
            cp.start()

    scale = HEAD_DIM ** -0.5
    rows = 8 * t_new
    ri = lax.broadcasted_iota(jnp.int32, (rows, 1), 0)
    tok = ri // 8
    hd = ri % 8
    slope = jnp.zeros((rows, 1), F32)
    for h in range(N_HEADS):
        slope = jnp.where(hd == h, float(slopes[h]), slope)

    def seg(t, off):
        return zs_ref[t, pl.ds(b, 1), off:off + W_ATT]

    qa = _expand_heads([seg(t, OFF_QA) * scale for t in range(t_new)])
    qb = _expand_heads([seg(t, OFF_QB) * scale for t in range(t_new)])
    upper2 = _upper2(MOBA_BLOCK)

    sa, ys = [], []
    for s in range(t_new):
        sc = jnp.sum(qa * seg(s, OFF_KA), axis=1, keepdims=True)
        sc = sc - slope * (tok - s).astype(F32)
        sa.append(jnp.where(tok >= s, sc, -jnp.inf))
        ys.append(jnp.sum(qb * seg(s, OFF_KB), axis=1, keepdims=True) * LOG2E)
    m = sa[0]
    for s in range(1, t_new):
        m = jnp.maximum(m, sa[s])
    l = jnp.zeros((rows, 1), F32)
    acc = jnp.zeros((rows, W_ATT), F32)
    for s in range(t_new):
        pr = jnp.exp(sa[s] - m)
        l = l + pr
        acc = acc + pr * seg(s, OFF_VA)
    parts = [(m, l, acc)]
    carry = jnp.zeros((rows, 1), F32)
    accb = jnp.zeros((rows, W_ATT), F32)
    for s in range(t_new - 1, -1, -1):
        valid = tok > s
        sp = _softplus2(ys[s])
        a = jnp.where(valid, jnp.exp2(ys[s] - sp + carry), 0.0)
        accb = accb + a * seg(s, OFF_VB)
        carry = carry - jnp.where(valid, sp, 0.0)
    qa_b = qa.astype(BF16)
    qb_b = qb.astype(BF16)
    kcol = lax.broadcasted_iota(jnp.int32, (rows, MOBA_BLOCK), 1)

    gates = []
    for blk in range(n_blk):
        slot = (b % 2) * n_blk + blk
        raw = _dot(qa_b, ka_buf[slot].astype(BF16))
        gates.append(jnp.sum(raw, axis=1, keepdims=True))
        dist = (past_len + tok - blk * MOBA_BLOCK - kcol).astype(F32)
        sc = raw - slope * dist
        m = jnp.max(sc, axis=1, keepdims=True)
        pr = jnp.exp(sc - m)
        parts.append((m, jnp.sum(pr, axis=1, keepdims=True),
                      _dot_nt(pr.astype(BF16), va_buf[slot].astype(BF16))))

    def sb_block(slot, carry, accb):
        y = _dot(qb_b, kb_buf[slot].astype(BF16)) * LOG2E
        sp = _softplus2(y)
        lk = -sp
        aft = _suffix_sums(lk, upper2)
        a = jnp.exp2(y - sp + aft + carry)
        accb = accb + _dot_nt(a.astype(BF16), vb_buf[slot].astype(BF16))
        return carry + aft[:, 0:1] + lk[:, 0:1], accb

    def alive(carry):
        return (jnp.max(carry) > SB_EXIT2).astype(jnp.int32)

    carry, accb = sb_block(b % 2, carry, accb)

    def older_cond(c):
        return (c[0] < n_blk) & (c[1] > 0)

    def older_body(c):
        cps = b_copies(b, c[0], B_SLOTS - 1)
        for cp in cps:
            cp.start()
        for cp in cps:
            cp.wait()
        carry, accb = sb_block(B_SLOTS - 1, c[2], c[3])
        return c[0] + 1, alive(carry), carry, accb

    accb = lax.while_loop(older_cond, older_body, (jnp.int32(1), alive(carry), carry, accb))[3]

    sel = []
    for nb in range(n_blk):
        rank = jnp.zeros((rows, 1), jnp.int32)
        for mth in range(n_blk):
            if mth < nb:
                rank += jnp.where(gates[mth] >= gates[nb], 1, 0)
            elif mth > nb:
                rank += jnp.where(gates[mth] > gates[nb], 1, 0)
        sel.append(rank < MOBA_TOPK)
    big = parts[0][0]
    for nb in range(n_blk):
        big = jnp.maximum(big, jnp.where(sel[nb], parts[nb + 1][0], -jnp.inf))
    w = jnp.exp(parts[0][0] - big)
    l = w * parts[0][1]
    acc = w * parts[0][2]
    for nb in range(n_blk):
        m_nb, l_nb, acc_nb = parts[nb + 1]
        w = jnp.where(sel[nb], jnp.exp(jnp.minimum(m_nb - big, 0.0)), 0.0)
        l = l + w * l_nb
        acc = acc + w * acc_nb
    _reduce_heads(acc / l, oa_ref, b)
    _reduce_heads(accb, ob_ref, b)


def _sample_attn(zs3, caches, layer, page_table, slopes):
    t_new, bsz, _ = zs3.shape
    n_pages = page_table.shape[1]
    n_blk = n_pages // PAGES_PER_BLK
    assert n_pages % PAGES_PER_BLK == 0
    hbm = pl.BlockSpec(memory_space=pl.ANY)
    blk_buf = lambda n: pltpu.VMEM((n, W_ATT, MOBA_BLOCK), F32)
    out_spec = pl.BlockSpec((t_new, bsz, W_ATT), lambda b, pt: (0, 0, 0))
    grid_spec = pltpu.PrefetchScalarGridSpec(
        num_scalar_prefetch=1, grid=(bsz,),
        in_specs=[pl.BlockSpec((t_new, bsz, IN_WIDTH), lambda b, pt: (0, 0, 0)),
                  hbm, hbm, hbm, hbm],
        out_specs=[out_spec, out_spec],
        scratch_shapes=[blk_buf(2 * n_blk), blk_buf(2 * n_blk),
                        blk_buf(B_SLOTS), blk_buf(B_SLOTS),
                        pltpu.SemaphoreType.DMA((2 * n_blk, 2 * PAGES_PER_BLK)),
                        pltpu.SemaphoreType.DMA((B_SLOTS, 2 * PAGES_PER_BLK))])
    body = functools.partial(_sample_attn_body, layer=layer, bsz=bsz, n_pages=n_pages,
                             t_new=t_new, slopes=tuple(float(s) for s in slopes))
    return pl.pallas_call(
        body, grid_spec=grid_spec,
        out_shape=[jax.ShapeDtypeStruct((t_new, bsz, W_ATT), F32)] * 2,
        compiler_params=_params(("arbitrary",)), name="sample_attn")(
            page_table.reshape(-1), zs3, *caches)


def _gate_matrix(kmean_t):
    rows = jnp.arange(W_ATT)[:, None] // HEAD_DIM
    cols = jnp.arange(LANES)[None, :] // 16
    tiled = jnp.tile(kmean_t[:, :, :16], (1, 1, LANES // 16))
    return jnp.swapaxes(jnp.where(rows == cols, tiled, 0.0), 1, 2)


def kernel(x_prompt, x_sample, cache_k_a, cache_v_a, cache_k_b, cache_v_b, page_table, ln_in_g, ln_in_b, w_in, w_out, w_spatial, b_spatial, ln_c_g, ln_c_b, ln1_g, ln1_b, w_ff1, w_ff2, ln2_g, ln2_b):
    batch, seq, _ = x_prompt.shape
    dec_batch, dec_seq, _ = x_sample.shape
    depth = w_in.shape[0]
    alpha = (2 * depth) ** 0.25
    slopes_np = _alibi_slopes(N_HEADS)
    slopes = jnp.asarray(slopes_np)
    n_blk = seq // MOBA_BLOCK
    assert n_blk <= 16 and seq % 512 == 0 and dec_seq <= CHUNK and dec_batch % 8 == 0
    assert all(math.frexp(float(s))[0] == 0.5 for s in slopes_np)

    w_in_b = w_in.astype(BF16)
    w_q = jnp.concatenate([w_in_b[:, :, OFF_QA:OFF_QA + W_ATT], w_in_b[:, :, OFF_QB:OFF_QB + W_ATT],
                           w_in_b[:, :, OFF_UC:]], axis=2)
    w_kv_t = jnp.swapaxes(jnp.concatenate(
        [w_in_b[:, :, OFF_KA:OFF_KA + 2 * W_ATT], w_in_b[:, :, OFF_KB:OFF_KB + 2 * W_ATT]],
        axis=2), 1, 2)
    w_out_b = w_out.astype(BF16)
    w_ff1_b = w_ff1.astype(BF16)
    w_ff2_b = w_ff2.astype(BF16)
    caches = [jnp.transpose(c, (0, 1, 3, 4, 2)).reshape(c.shape[0], c.shape[1], W_ATT, PAGE)
              for c in (cache_k_a, cache_v_a, cache_k_b, cache_v_b)]

    bias_p = jnp.repeat(jnp.swapaxes(b_spatial, 1, 2), HEAD_DIM, axis=2)
    coef_s = jnp.repeat(jnp.transpose(w_spatial[:, :, :dec_seq, :dec_seq], (0, 2, 3, 1)),
                        HEAD_DIM, axis=3).reshape(depth, dec_seq * dec_seq, W_C)
    bias_s = jnp.repeat(jnp.swapaxes(b_spatial[:, :, :dec_seq], 1, 2), HEAD_DIM, axis=2)

    hp = x_prompt.reshape(batch * seq, D_MODEL)
    hs = jnp.swapaxes(x_sample, 0, 1).reshape(dec_seq * dec_batch, D_MODEL)
    kv_p = [jnp.zeros((depth, batch, W_ATT, seq), F32) for _ in range(4)]
    kv_s = [jnp.zeros((depth, dec_seq, W_ATT, dec_batch), F32) for _ in range(4)]
    vc_s = []
    for l in range(depth):
        ln = (ln_in_g, ln_in_b) if l == 0 else None
        lnc_g, lnc_b = ln_c_g[l].reshape(-1), ln_c_b[l].reshape(-1)

        hp, zb, kv_p, kmean_t = _in_proj_prompt(hp, w_q, w_kv_t, ln, kv_p, l, depth, batch, seq)
        selb = _moba_gate(zb, _gate_matrix(kmean_t), batch, seq)
        mix_a = _moba_prompt(zb, kv_p[0], kv_p[1], selb, slopes, l, batch, seq)
        mix_b = _sb_prompt(zb, kv_p[2], kv_p[3], l, batch, seq)
        mix_c = _gmlp_prompt(zb, w_spatial[l], bias_p[l], lnc_g, lnc_b)
        h1 = _out_proj(mix_a, mix_b, mix_c, hp, w_out_b, l, ln1_g[l], ln1_b[l], alpha)
        hp = _ffn(h1, w_ff1_b, w_ff2_b, l, ln2_g[l], ln2_b[l], alpha)

        hs, zs, kv_s = _in_proj_sample(hs, w_in_b, w_kv_t, ln, kv_s, l, depth, dec_seq, dec_batch)
        oa, ob = _sample_attn(zs.reshape(dec_seq, dec_batch, IN_WIDTH), caches, l,
                              page_table, slopes_np)
        mix_c, vn = _gmlp_sample(zs, coef_s[l], bias_s[l], lnc_g, lnc_b, dec_seq, dec_batch)
        vc_s.append(vn)
        h1 = _out_proj(oa.reshape(-1, W_ATT), ob.reshape(-1, W_ATT), mix_c, hs,
                       w_out_b, l, ln1_g[l], ln1_b[l], alpha)
        hs = _ffn(h1, w_ff1_b, w_ff2_b, l, ln2_g[l], ln2_b[l], alpha)

    kv_out_p = [jnp.transpose(a.reshape(depth, batch, N_HEADS, HEAD_DIM, seq), (0, 1, 4, 2, 3))
                for a in kv_p]
    kv_out_s = [jnp.transpose(a.reshape(depth, dec_seq, N_HEADS, HEAD_DIM, dec_batch), (0, 4, 1, 2, 3))
                for a in kv_s]
    vc_out = jnp.swapaxes(jnp.stack(vc_s).reshape(depth, dec_seq, dec_batch, W_C), 1, 2)
    y_s = jnp.swapaxes(hs.reshape(dec_seq, dec_batch, D_MODEL), 0, 1)
    return (hp.reshape(batch, seq, D_MODEL), y_s, *kv_out_p, *kv_out_s, vc_out)
```

```python
import functools
import math

import jax
import jax.numpy as jnp
import numpy as np
from jax import lax
from jax.experimental import pallas as pl
from jax.experimental.pallas import tpu as pltpu

F32 = jnp.float32
BF16 = jnp.bfloat16

D_MODEL = 1024
HEAD_DIM = 64
N_HEADS = 6
W_ATT = N_HEADS * HEAD_DIM
C_GROUPS = 4
W_C = C_GROUPS * HEAD_DIM
IN_WIDTH = 6 * W_ATT + 2 * W_C
D_FF = 4 * D_MODEL
MOBA_BLOCK = 256
MOBA_TOPK = 3
CHUNK = 128
PAGE = 128
LN_EPS = 1e-5
LANES = 128
NEG_BIG = -1e30
SB_EXIT = -110.0

OFF_QA, OFF_KA, OFF_VA = 0, W_ATT, 2 * W_ATT
OFF_QB, OFF_KB, OFF_VB = 3 * W_ATT, 4 * W_ATT, 5 * W_ATT
OFF_UC, OFF_VC = 6 * W_ATT, 6 * W_ATT + W_C
ZP_QA, ZP_QB, ZP_UC, ZP_VC = 0, W_ATT, 2 * W_ATT, 2 * W_ATT + W_C
ZP_WIDTH = 2 * W_ATT + 2 * W_C

VMEM_LIMIT = 56 * 1024 * 1024


def _alibi_slopes(n):
    def pow2(m):
        start = 2.0 ** (-8.0 / m)
        return [start ** (i + 1) for i in range(m)]
    p = 2 ** int(math.floor(math.log2(n)))
    s = pow2(p)
    if p < n:
        s = s + pow2(2 * p)[0::2][: n - p]
    return np.array(s, dtype=np.float32)


def _ln(x, g, b):
    mu = jnp.mean(x, axis=-1, keepdims=True)
    d = x - mu
    var = jnp.mean(d * d, axis=-1, keepdims=True)
    return d * lax.rsqrt(var + LN_EPS) * g + b


def _dot(a, b, precision=None):
    return jnp.dot(a, b, preferred_element_type=F32, precision=precision)


def _dot_nt(a, b, precision=None):
    return lax.dot_general(a, b, (((1,), (1,)), ((), ())),
                           preferred_element_type=F32, precision=precision)


def _params(sem):
    return pltpu.CompilerParams(dimension_semantics=sem, vmem_limit_bytes=VMEM_LIMIT)


def _in_proj_prompt_body(*refs, apply_ln, tm):
    it = iter(refs)
    x_ref = next(it)
    if apply_ln:
        g_ref, b_ref = next(it), next(it)
    wq_ref, wkv_ref = next(it), next(it)
    for _ in range(4):
        next(it)
    if apply_ln:
        h_ref = next(it)
    zb_ref = next(it)
    kv_refs = [next(it) for _ in range(4)]
    km_ref = next(it)
    ti = pl.program_id(1)

    x = x_ref[...]
    if apply_ln:
        x = _ln(x, g_ref[...], b_ref[...])
        h_ref[...] = x
    hb = x.astype(BF16)
    for c in range(0, ZP_WIDTH, 256):
        zb_ref[:, c:c + 256] = _dot(hb, wq_ref[:, c:c + 256]).astype(BF16)
    lane = lax.broadcasted_iota(jnp.int32, (W_ATT, LANES), 1)

    @pl.when(ti == 0)
    def _zero_means():
        km_ref[...] = jnp.zeros_like(km_ref)

    kt_all = _dot_nt(wkv_ref[...], hb)
    for t in range(4):
        kt = kt_all[t * W_ATT:(t + 1) * W_ATT]
        kv_refs[t][...] = kt
        if t == 0:
            km = km_ref[...]
            for r in range(tm // MOBA_BLOCK):
                mean = jnp.sum(kt[:, r * MOBA_BLOCK:(r + 1) * MOBA_BLOCK], axis=1,
                               keepdims=True) * (1.0 / MOBA_BLOCK)
                km = jnp.where(lane == ti * (tm // MOBA_BLOCK) + r, mean, km)
            km_ref[...] = km


def _in_proj_prompt(x, wq, wkv, ln, kv_prev, layer, depth, batch, seq):
    n = x.shape[0]
    tm = 512
    nt = seq // tm
    row = lambda b, t: (b * nt + t, 0)
    const = lambda b, t: (0, 0)
    args, in_specs = [x], [pl.BlockSpec((tm, D_MODEL), row)]
    if ln is not None:
        args += [ln[0].reshape(1, D_MODEL), ln[1].reshape(1, D_MODEL)]
        in_specs += [pl.BlockSpec((1, D_MODEL), const)] * 2
    args += [wq, wkv]
    in_specs += [pl.BlockSpec((None, D_MODEL, ZP_WIDTH), lambda b, t: (layer, 0, 0)),
                 pl.BlockSpec((None, 4 * W_ATT, D_MODEL), lambda b, t: (layer, 0, 0))]
    out_shape, out_specs = [], []
    if ln is not None:
        out_shape.append(jax.ShapeDtypeStruct((n, D_MODEL), F32))
        out_specs.append(pl.BlockSpec((tm, D_MODEL), row))
    out_shape.append(jax.ShapeDtypeStruct((n, ZP_WIDTH), BF16))
    out_specs.append(pl.BlockSpec((tm, ZP_WIDTH), row))
    aliases = {}
    for t in range(4):
        aliases[len(args)] = len(out_shape)
        args.append(kv_prev[t])
        in_specs.append(pl.BlockSpec(memory_space=pl.ANY))
        out_shape.append(jax.ShapeDtypeStruct((depth, batch, W_ATT, seq), F32))
        out_specs.append(pl.BlockSpec((None, None, W_ATT, tm), lambda b, t: (layer, b, 0, t)))
    out_shape.append(jax.ShapeDtypeStruct((batch, W_ATT, LANES), F32))
    out_specs.append(pl.BlockSpec((None, W_ATT, LANES), lambda b, t: (b, 0, 0)))
    body = functools.partial(_in_proj_prompt_body, apply_ln=ln is not None, tm=tm)
    outs = list(pl.pallas_call(
        body, grid=(batch, nt), in_specs=in_specs, out_specs=out_specs, out_shape=out_shape,
        input_output_aliases=aliases, compiler_params=_params(("parallel", "arbitrary")),
        name="in_proj_prompt")(*args))
    h = outs.pop(0) if ln is not None else x
    return h, outs[0], outs[1:5], outs[5]


def _in_proj_sample_body(*refs, apply_ln, t_new, bsz):
    it = iter(refs)
    x_ref = next(it)
    if apply_ln:
        g_ref, b_ref = next(it), next(it)
    w_ref, wkv_ref = next(it), next(it)
    for _ in range(4):
        next(it)
    if apply_ln:
        h_ref = next(it)
    zs_ref = next(it)
    kv_refs = [next(it) for _ in range(4)]

    x = x_ref[...]
    if apply_ln:
        x = _ln(x, g_ref[...], b_ref[...])
        h_ref[...] = x
    hb = x.astype(BF16)
    for c in range(0, IN_WIDTH, 256):
        zs_ref[:, c:c + 256] = _dot(hb, w_ref[:, c:c + 256])
    for t in range(4):
        kt = _dot_nt(wkv_ref[t * W_ATT:(t + 1) * W_ATT, :], hb)
        for tt in range(t_new):
            kv_refs[t][tt] = kt[:, tt * bsz:(tt + 1) * bsz]


def _in_proj_sample(x, w, wkv, ln, kv_prev, layer, depth, t_new, bsz):
    n = x.shape[0]
    const = lambda i: (0, 0)
    args, in_specs = [x], [pl.BlockSpec((n, D_MODEL), const)]
    if ln is not None:
        args += [ln[0].reshape(1, D_MODEL), ln[1].reshape(1, D_MODEL)]
        in_specs += [pl.BlockSpec((1, D_MODEL), const)] * 2
    args += [w, wkv]
    in_specs += [pl.BlockSpec((None, D_MODEL, IN_WIDTH), lambda i: (layer, 0, 0)),
                 pl.BlockSpec((None, 4 * W_ATT, D_MODEL), lambda i: (layer, 0, 0))]
    out_shape, out_specs = [], []
    if ln is not None:
        out_shape.append(jax.ShapeDtypeStruct((n, D_MODEL), F32))
        out_specs.append(pl.BlockSpec((n, D_MODEL), const))
    out_shape.append(jax.ShapeDtypeStruct((n, IN_WIDTH), F32))
    out_specs.append(pl.BlockSpec((n, IN_WIDTH), const))
    aliases = {}
    for t in range(4):
        aliases[len(args)] = len(out_shape)
        args.append(kv_prev[t])
        in_specs.append(pl.BlockSpec(memory_space=pl.ANY))
        out_shape.append(jax.ShapeDtypeStruct((depth, t_new, W_ATT, bsz), F32))
        out_specs.append(pl.BlockSpec((None, t_new, W_ATT, bsz), lambda i: (layer, 0, 0, 0)))
    body = functools.partial(_in_proj_sample_body, apply_ln=ln is not None, t_new=t_new, bsz=bsz)
    outs = list(pl.pallas_call(
        body, grid=(1,), in_specs=in_specs, out_specs=out_specs, out_shape=out_shape,
        input_output_aliases=aliases, compiler_params=_params(("arbitrary",)),
        name="in_proj_sample")(*args))
    h = outs.pop(0) if ln is not None else x
    return h, outs[0], outs[1:5]


def _gate_body(q_ref, kmt_ref, o_ref):
    cur = pl.program_id(1)
    g = _dot_nt(kmt_ref[...], q_ref[...].astype(F32), precision=lax.Precision.HIGHEST)
    blk = lax.broadcasted_iota(jnp.int32, g.shape, 0) & 15
    past = blk < cur
    g = jnp.where(past, g, -jnp.inf)
    rank = jnp.zeros(g.shape, jnp.int32)
    for s in range(1, 16):
        lower = pltpu.roll(g, s, axis=0)
        rank += jnp.where(lower >= g, jnp.where(blk >= s, 1, 0), 0)
        upper = pltpu.roll(g, LANES - s, axis=0)
        rank += jnp.where(upper > g, jnp.where(blk + s <= 15, 1, 0), 0)
    sel = jnp.where(past, jnp.where(rank < MOBA_TOPK, 0.0, NEG_BIG), NEG_BIG)
    o_ref[...] = sel.T


def _moba_gate(zb, kmt, batch, seq):
    nq = seq // MOBA_BLOCK
    return pl.pallas_call(
        _gate_body, grid=(batch, nq),
        in_specs=[pl.BlockSpec((MOBA_BLOCK, W_ATT), lambda b, i: (b * nq + i, ZP_QA // W_ATT)),
                  pl.BlockSpec((None, LANES, W_ATT), lambda b, i: (b, 0, 0))],
        out_specs=pl.BlockSpec((MOBA_BLOCK, LANES), lambda b, i: (b * nq + i, 0)),
        out_shape=jax.ShapeDtypeStruct((batch * seq, LANES), F32),
        compiler_params=_params(("parallel", "parallel")), name="moba_gate")(zb, kmt)


ROLE_SEL0, ROLE_SEL1 = 0, 1
ROLE_ROW, ROLE_KEY = 96, 97
ROLE_BLK, ROLE_2ND = 98, 99


ATT_PAIRS = 3


def _moba_body(slopes_ref, q_ref, kt_ref, vt_ref, selb_ref, o_ref, lhs_s, rhs_s):
    pg = pl.program_id(1)
    i = pl.program_id(2)
    tq = MOBA_BLOCK
    selb = selb_ref[...]
    lane = lax.broadcasted_iota(jnp.int32, (2 * tq, LANES), 1)
    row2 = lax.broadcasted_iota(jnp.int32, (2 * tq, LANES), 0)
    head1 = row2 >= tq
    rowf = jnp.where(head1, row2 - tq, row2).astype(F32)
    role = lax.broadcasted_iota(jnp.int32, (LANES, 2 * tq), 0)
    key2 = lax.broadcasted_iota(jnp.int32, (LANES, 2 * tq), 1)
    second = key2 >= tq
    keyf = jnp.where(second, key2 - tq, key2).astype(F32)
    rhs_roles = jnp.where(
        (role == ROLE_SEL0) & jnp.logical_not(second), 1.0,
        jnp.where(((role == ROLE_SEL1) | (role == ROLE_2ND)) & second, 1.0,
                  jnp.where((role == ROLE_ROW) | (role == ROLE_BLK), 1.0,
                            jnp.where(role == ROLE_KEY, keyf, 0.0)))).astype(BF16)
    rq = jnp.where(lax.broadcasted_iota(jnp.int32, (2 * tq, tq), 0) >= tq,
                   lax.broadcasted_iota(jnp.int32, (2 * tq, tq), 0) - tq,
                   lax.broadcasted_iota(jnp.int32, (2 * tq, tq), 0))
    ck = lax.broadcasted_iota(jnp.int32, (2 * tq, tq), 1)
    d0 = (rq - ck).astype(F32)
    causal = ck <= rq
    row0 = pl.multiple_of(i * tq, tq)
    fixed = (lane == ROLE_ROW) | (lane == ROLE_KEY) | (lane == ROLE_2ND)

    pairs = []
    for pr_i in range(ATT_PAIRS):
        lanes = slice(pr_i * LANES, (pr_i + 1) * LANES)
        rhs_s[pr_i, LANES:2 * LANES, :] = rhs_roles
        q = q_ref[:, lanes].astype(F32) * (HEAD_DIM ** -0.5)
        q2 = jnp.concatenate([q, q], axis=0)
        qh = jnp.where((lane >= HEAD_DIM) == head1, q2, 0.0).astype(BF16)
        lhs_s[pr_i, :, 0:LANES] = qh
        hd0 = 2 * (pg * ATT_PAIRS + pr_i)
        slope = jnp.where(lax.broadcasted_iota(jnp.int32, (2 * tq, 1), 0) >= tq,
                          slopes_ref[hd0 + 1], slopes_ref[hd0])
        lhs_fix = jnp.where(lane == ROLE_ROW, -slope * rowf,
                            jnp.where(lane == ROLE_KEY, slope, slope * tq))
        pairs.append((hd0, slope, lhs_fix, qh))

    def own_block():
        state = []
        for pr_i in range(ATT_PAIRS):
            lanes = slice(pr_i * LANES, (pr_i + 1) * LANES)
            _, slope, _, qh = pairs[pr_i]
            kd = kt_ref[lanes, pl.ds(row0, tq)].astype(BF16)
            vd = vt_ref[lanes, pl.ds(row0, tq)].astype(BF16)
            s = jnp.where(causal, _dot(qh, kd) - slope * d0, -jnp.inf)
            m = jnp.max(s, axis=1, keepdims=True)
            pr = jnp.exp(s - m)
            state += [m, jnp.sum(pr, axis=1, keepdims=True), _dot_nt(pr.astype(BF16), vd)]
        return tuple(state)

    def body(pp, carry):
        j0 = 2 * pp
        r0 = pl.multiple_of(j0 * tq, 2 * tq)
        out = []
        for pr_i in range(ATT_PAIRS):
            lanes = slice(pr_i * LANES, (pr_i + 1) * LANES)
            hd0, slope, lhs_fix, _ = pairs[pr_i]
            m, l, acc = carry[3 * pr_i:3 * pr_i + 3]
            rhs_s[pr_i, 0:LANES, :] = kt_ref[lanes, pl.ds(r0, 2 * tq)].astype(BF16)
            v2 = vt_ref[lanes, pl.ds(r0, 2 * tq)].astype(BF16)
            sel = jnp.concatenate(
                [pltpu.roll(selb, (LANES - (hd0 + hh) * 16 - j0) % LANES, axis=1) for hh in range(2)],
                axis=0)
            off = -slope * ((i - j0) * tq).astype(F32)
            ext = jnp.where(fixed, lhs_fix, jnp.where(lane == ROLE_BLK, off, sel))
            lhs_s[pr_i, :, LANES:2 * LANES] = ext.astype(BF16)
            t = _dot(lhs_s[pr_i], rhs_s[pr_i])
            m_new = jnp.maximum(m, jnp.max(t, axis=1, keepdims=True))
            alpha = jnp.exp(m - m_new)
            pr = jnp.exp(t - m_new)
            l = alpha * l + jnp.sum(pr, axis=1, keepdims=True)
            acc = alpha * acc + _dot_nt(pr.astype(BF16), v2)
            out += [m_new, l, acc]
        return tuple(out)

    state = lax.fori_loop(0, (i + 1) // 2, body, own_block())
    for pr_i in range(ATT_PAIRS):
        o = state[3 * pr_i + 2] / state[3 * pr_i + 1]
        o_ref[:, pr_i * LANES:(pr_i + 1) * LANES] = jnp.where(
            lax.broadcasted_iota(jnp.int32, (tq, LANES), 1) < HEAD_DIM, o[:tq], o[tq:]).astype(o_ref.dtype)


def _moba_prompt(zb, kt, vt, selb, slopes, layer, batch, seq):
    nq = seq // MOBA_BLOCK
    wq = ATT_PAIRS * LANES
    cq = ZP_QA // wq
    kv_spec = pl.BlockSpec((None, None, wq, seq), lambda b, p, i, s: (layer, b, p, 0))
    grid_spec = pltpu.PrefetchScalarGridSpec(
        num_scalar_prefetch=1, grid=(batch, N_HEADS // 2 // ATT_PAIRS, nq),
        in_specs=[pl.BlockSpec((MOBA_BLOCK, wq), lambda b, p, i, s: (b * nq + i, cq + p)),
                  kv_spec, kv_spec,
                  pl.BlockSpec((MOBA_BLOCK, LANES), lambda b, p, i, s: (b * nq + i, 0))],
        out_specs=pl.BlockSpec((MOBA_BLOCK, wq), lambda b, p, i, s: (b * nq + i, p)),
        scratch_shapes=[pltpu.VMEM((ATT_PAIRS, 2 * MOBA_BLOCK, 2 * LANES), BF16),
                        pltpu.VMEM((ATT_PAIRS, 2 * LANES, 2 * MOBA_BLOCK), BF16)])
    return pl.pallas_call(
        _moba_body, grid_spec=grid_spec,
        out_shape=jax.ShapeDtypeStruct((batch * seq, W_ATT), BF16),
        compiler_params=_params(("parallel", "parallel", "arbitrary")),
        name="moba_prompt")(slopes, zb, kt, vt, selb)


LOG2E = 1.4426950408889634
SB_EXIT2 = SB_EXIT * LOG2E


def _softplus2(y):
    return jnp.maximum(y, 0.0) + jnp.log2(1.0 + jnp.exp2(-jnp.abs(y)))


def _suffix_sums(lk, upper2):
    hi = lk.astype(BF16)
    lo = (lk - hi.astype(F32)).astype(BF16)
    return _dot(jnp.concatenate([hi, lo], axis=1), upper2)


def _upper2(n):
    r = lax.broadcasted_iota(jnp.int32, (2 * n, n), 0)
    c = lax.broadcasted_iota(jnp.int32, (2 * n, n), 1)
    return jnp.where(jnp.where(r >= n, r - n, r) > c, 1.0, 0.0).astype(BF16)


def _sb_body(q_ref, kt_ref, vt_ref, o_ref):
    i = pl.program_id(2)
    tq = MOBA_BLOCK
    lane = lax.broadcasted_iota(jnp.int32, (2 * tq, LANES), 1)
    row2 = lax.broadcasted_iota(jnp.int32, (2 * tq, tq), 0)
    rq = jnp.where(row2 >= tq, row2 - tq, row2)
    ck = lax.broadcasted_iota(jnp.int32, (2 * tq, tq), 1)
    before = ck < rq
    upper2 = _upper2(tq)
    head1 = lax.broadcasted_iota(jnp.int32, (2 * tq, LANES), 0) >= tq
    qhs = []
    for pr_i in range(ATT_PAIRS):
        q = q_ref[:, pr_i * LANES:(pr_i + 1) * LANES].astype(F32) * (HEAD_DIM ** -0.5)
        q2 = jnp.concatenate([q, q], axis=0)
        qhs.append(jnp.where((lane >= HEAD_DIM) == head1, q2, 0.0).astype(BF16))

    def block(j, st, diag):
        r0 = pl.multiple_of(j * tq, tq)
        ys, sps, lks, vbs = [], [], [], []
        for pr_i in range(ATT_PAIRS):
            lanes = slice(pr_i * LANES, (pr_i + 1) * LANES)
            kb = kt_ref[lanes, pl.ds(r0, tq)].astype(BF16)
            vbs.append(vt_ref[lanes, pl.ds(r0, tq)].astype(BF16))
            y = _dot(qhs[pr_i], kb) * LOG2E
            sp = _softplus2(y)
            lk = -sp
            if diag:
                lk = jnp.where(before, lk, 0.0)
            ys.append(y)
            sps.append(sp)
            lks.append(lk)
        aft_all = _suffix_sums(jnp.concatenate(lks, axis=0), upper2)
        out = []
        for pr_i in range(ATT_PAIRS):
            carry, acc = st[2 * pr_i], st[2 * pr_i + 1]
            aft = aft_all[pr_i * 2 * tq:(pr_i + 1) * 2 * tq]
            a = jnp.exp2(ys[pr_i] - sps[pr_i] + aft + carry)
            if diag:
                a = jnp.where(before, a, 0.0)
            out += [carry + aft[:, 0:1] + lks[pr_i][:, 0:1],
                    acc + _dot_nt(a.astype(BF16), vbs[pr_i])]
        return tuple(out)

    def alive(st):
        top = st[0]
        for pr_i in range(1, ATT_PAIRS):
            top = jnp.maximum(top, st[2 * pr_i])
        return (jnp.max(top) > SB_EXIT2).astype(jnp.int32)

    zc, za = jnp.zeros((2 * tq, 1), F32), jnp.zeros((2 * tq, LANES), F32)
    init = (zc, za) * ATT_PAIRS
    st = lax.cond(i >= 1,
                  lambda: block(i - 1, block(i, init, True), False),
                  lambda: block(i, init, True))

    def cond(c):
        return (c[0] >= 0) & (c[1] > 0)

    def body(c):
        st = block(c[0], c[2:], False)
        return (c[0] - 1, alive(st)) + st

    res = lax.while_loop(cond, body, (i - 2, alive(st)) + st)
    for pr_i in range(ATT_PAIRS):
        acc = res[2 + 2 * pr_i + 1]
        o_ref[:, pr_i * LANES:(pr_i + 1) * LANES] = jnp.where(
            lax.broadcasted_iota(jnp.int32, (tq, LANES), 1) < HEAD_DIM, acc[:tq], acc[tq:]).astype(o_ref.dtype)


def _sb_prompt(zb, kt, vt, layer, batch, seq):
    nq = seq // MOBA_BLOCK
    wq = ATT_PAIRS * LANES
    cq = ZP_QB // wq
    kv_spec = pl.BlockSpec((None, None, wq, seq), lambda b, p, i: (layer, b, p, 0))
    return pl.pallas_call(
        _sb_body, grid=(batch, N_HEADS // 2 // ATT_PAIRS, nq),
        in_specs=[pl.BlockSpec((MOBA_BLOCK, wq), lambda b, p, i: (b * nq + i, cq + p)),
                  kv_spec, kv_spec],
        out_specs=pl.BlockSpec((MOBA_BLOCK, wq), lambda b, p, i: (b * nq + i, p)),
        out_shape=jax.ShapeDtypeStruct((batch * seq, W_ATT), BF16),
        compiler_params=_params(("parallel", "parallel", "arbitrary")),
        name="sb_prompt")(zb, kt, vt)


def _group_norm(v, g, b):
    gi = (lax.broadcasted_iota(jnp.int32, (3 * W_C, W_C), 0) % W_C) // HEAD_DIM
    gj = lax.broadcasted_iota(jnp.int32, (3 * W_C, W_C), 1) // HEAD_DIM
    gmean3 = jnp.where(gi == gj, 1.0 / HEAD_DIM, 0.0).astype(BF16)

    def group_mean(x):
        hi = x.astype(BF16)
        r = x - hi.astype(F32)
        mid = r.astype(BF16)
        lo = (r - mid.astype(F32)).astype(BF16)
        return _dot(jnp.concatenate([hi, mid, lo], axis=1), gmean3)

    d = v - group_mean(v)
    var = group_mean(d * d)
    return d * lax.rsqrt(var + LN_EPS) * g + b


def _gmlp_prompt_body(u_ref, v_ref, ws_ref, bias_ref, g_ref, b_ref, o_ref, *, rows):
    ti = lax.broadcasted_iota(jnp.int32, (CHUNK, CHUNK), 0)
    tj = lax.broadcasted_iota(jnp.int32, (CHUNK, CHUNK), 1)
    lane = lax.broadcasted_iota(jnp.int32, (CHUNK, LANES), 1)
    wpair = []
    for pr in range(2):
        w0 = jnp.where(tj <= ti, ws_ref[2 * pr], 0.0)
        w1 = jnp.where(tj <= ti, ws_ref[2 * pr + 1], 0.0)
        wpair.append(jnp.concatenate([w0, w1], axis=1).astype(BF16))
    vn_all = _group_norm(jax.nn.gelu(v_ref[...].astype(F32)), g_ref[...], b_ref[...])
    for r in range(rows // CHUNK):
        sl = slice(r * CHUNK, (r + 1) * CHUNK)
        u = jax.nn.gelu(u_ref[sl, :].astype(F32))
        vn = vn_all[sl, :]
        mixes = []
        for pr in range(2):
            vp = vn[:, pr * LANES:(pr + 1) * LANES]
            rhs = jnp.concatenate([jnp.where(lane < HEAD_DIM, vp, 0.0),
                                   jnp.where(lane >= HEAD_DIM, vp, 0.0)], axis=0).astype(BF16)
            mixes.append(_dot(wpair[pr], rhs))
        mix = jnp.concatenate(mixes, axis=1) + bias_ref[...]
        o_ref[sl, :] = (u * mix).astype(o_ref.dtype)


def _gmlp_prompt(zb, ws, bias, g, b):
    n = zb.shape[0]
    rows = 512
    return pl.pallas_call(
        functools.partial(_gmlp_prompt_body, rows=rows), grid=(n // rows,),
        in_specs=[pl.BlockSpec((rows, W_C), lambda i: (i, ZP_UC // W_C)),
                  pl.BlockSpec((rows, W_C), lambda i: (i, ZP_VC // W_C)),
                  pl.BlockSpec((C_GROUPS, CHUNK, CHUNK), lambda i: (0, 0, 0)),
                  pl.BlockSpec((CHUNK, W_C), lambda i: (0, 0)),
                  pl.BlockSpec((1, W_C), lambda i: (0, 0)),
                  pl.BlockSpec((1, W_C), lambda i: (0, 0))],
        out_specs=pl.BlockSpec((rows, W_C), lambda i: (i, 0)),
        out_shape=jax.ShapeDtypeStruct((n, W_C), BF16),
        compiler_params=_params(("parallel",)), name="gmlp_prompt")(
            zb, zb, ws, bias, g.reshape(1, W_C), b.reshape(1, W_C))


def _gmlp_sample_body(u_ref, v_ref, coef_ref, bias_ref, g_ref, b_ref, o_ref, vn_ref, *, t_new, bsz):
    vns = []
    for t in range(t_new):
        sl = slice(t * bsz, (t + 1) * bsz)
        vn = _group_norm(jax.nn.gelu(v_ref[sl, :]), g_ref[...], b_ref[...])
        vn_ref[sl, :] = vn
        vns.append(vn)
    for t in range(t_new):
        sl = slice(t * bsz, (t + 1) * bsz)
        mix = bias_ref[t:t + 1, :]
        for s in range(t + 1):
            mix = mix + coef_ref[t * t_new + s:t * t_new + s + 1, :] * vns[s]
        o_ref[sl, :] = (jax.nn.gelu(u_ref[sl, :]) * mix).astype(o_ref.dtype)


def _gmlp_sample(zs, coef, bias, g, b, t_new, bsz):
    n = zs.shape[0]
    full = lambda shape: pl.BlockSpec(shape, lambda i: (0, 0))
    return pl.pallas_call(
        functools.partial(_gmlp_sample_body, t_new=t_new, bsz=bsz), grid=(1,),
        in_specs=[pl.BlockSpec((n, W_C), lambda i: (0, OFF_UC // W_C)),
                  pl.BlockSpec((n, W_C), lambda i: (0, OFF_VC // W_C)),
                  full((t_new * t_new, W_C)), full((t_new, W_C)), full((1, W_C)), full((1, W_C))],
        out_specs=[full((n, W_C)), full((n, W_C))],
        out_shape=[jax.ShapeDtypeStruct((n, W_C), BF16), jax.ShapeDtypeStruct((n, W_C), F32)],
        compiler_params=_params(("arbitrary",)), name="gmlp_sample")(
            zs, zs, coef, bias, g.reshape(1, W_C), b.reshape(1, W_C))


def _out_ffn_body(ma_ref, mb_ref, mc_ref, h_ref, wo_ref, g1_ref, b1_ref,
                  w1_ref, w2_ref, g_ref, b_ref, o_ref, acc_ref, *, alpha):
    mixed = jnp.concatenate([ma_ref[...].astype(BF16), mb_ref[...].astype(BF16),
                             mc_ref[...].astype(BF16)], axis=1)
    h = _ln(alpha * h_ref[...] + _dot(mixed, wo_ref[...]), g1_ref[...], b1_ref[...])
    hb = h.astype(BF16)
    step = 512
    for c in range(0, D_FF, step):
        a = _dot(hb, w1_ref[:, c:c + step])
        a = jnp.square(jnp.maximum(a, 0.0)).astype(BF16)
        y = _dot(a, w2_ref[c:c + step, :])
        if c == 0:
            acc_ref[...] = y
        else:
            acc_ref[...] += y
    o_ref[...] = _ln(alpha * h + acc_ref[...], g_ref[...], b_ref[...])


def _out_ffn(ma, mb, mc, h, wo, w1, w2, layer, g1, b1, g2, b2, alpha):
    n = h.shape[0]
    tm = min(512, n)
    row = lambda i: (i, 0)
    const = lambda i: (0, 0)
    lay = lambda i: (layer, 0, 0)
    vec = pl.BlockSpec((1, D_MODEL), const)
    return pl.pallas_call(
        functools.partial(_out_ffn_body, alpha=alpha), grid=(n // tm,),
        in_specs=[pl.BlockSpec((tm, W_ATT), row), pl.BlockSpec((tm, W_ATT), row),
                  pl.BlockSpec((tm, W_C), row), pl.BlockSpec((tm, D_MODEL), row),
                  pl.BlockSpec((None, D_MODEL, D_MODEL), lay, pipeline_mode=pl.Buffered(1)),
                  vec, vec,
                  pl.BlockSpec((None, D_MODEL, D_FF), lay, pipeline_mode=pl.Buffered(1)),
                  pl.BlockSpec((None, D_FF, D_MODEL), lay, pipeline_mode=pl.Buffered(1)),
                  vec, vec],
        out_specs=pl.BlockSpec((tm, D_MODEL), row),
        out_shape=jax.ShapeDtypeStruct((n, D_MODEL), F32),
        scratch_shapes=[pltpu.VMEM((tm, D_MODEL), F32)],
        compiler_params=_params(("parallel",)), name="out_ffn")(
            ma, mb, mc, h, wo, g1.reshape(1, D_MODEL), b1.reshape(1, D_MODEL),
            w1, w2, g2.reshape(1, D_MODEL), b2.reshape(1, D_MODEL))


PAGES_PER_BLK = MOBA_BLOCK // PAGE
B_SLOTS = 3


def _expand_heads(q):
    r = lax.broadcasted_iota(jnp.int32, (8, W_ATT), 0)
    c = lax.broadcasted_iota(jnp.int32, (8, W_ATT), 1) // HEAD_DIM
    pieces = [jnp.where(r == c, jnp.broadcast_to(row, (8, W_ATT)), 0.0) for row in q]
    return jnp.concatenate(pieces, axis=0)


def _reduce_heads(o, out_ref, b):
    r = lax.broadcasted_iota(jnp.int32, (8, W_ATT), 0)
    c = lax.broadcasted_iota(jnp.int32, (8, W_ATT), 1) // HEAD_DIM
    for t in range(o.shape[0] // 8):
        piece = jnp.where(r == c, o[8 * t:8 * t + 8, :], 0.0)
        out_ref[t, pl.ds(b, 1), :] = jnp.sum(piece, axis=0, keepdims=True)


def _sample_attn_body(pt_ref, zs_ref, cka, cva, ckb, cvb, oa_ref, ob_ref,
                      ka_buf, va_buf, kb_buf, vb_buf, sem_a, sem_b,
                      *, layer, bsz, n_pages, t_new, slopes):
    b = pl.program_id(0)
    past_len = n_pages * PAGE
    n_blk = n_pages // PAGES_PER_BLK

    def blk_copies(src_k, src_v, dst_k, dst_v, sem, first_page, slot):
        cps = []
        for k in range(PAGES_PER_BLK):
            pg = pt_ref[first_page + k]
            win = pl.ds(k * PAGE, PAGE)
            cps.append(pltpu.make_async_copy(src_k.at[layer, pg], dst_k.at[slot, :, win],
                                             sem.at[slot, 2 * k]))
            cps.append(pltpu.make_async_copy(src_v.at[layer, pg], dst_v.at[slot, :, win],
                                             sem.at[slot, 2 * k + 1]))
        return cps

    def a_copies(seq):
        cps = []
        for k in range(n_blk):
            cps += blk_copies(cka, cva, ka_buf, va_buf, sem_a,
                              seq * n_pages + k * PAGES_PER_BLK, (seq % 2) * n_blk + k)
        return cps

    def b_copies(seq, back, slot):
        return blk_copies(ckb, cvb, kb_buf, vb_buf, sem_b,
                          seq * n_pages + (n_blk - 1 - back) * PAGES_PER_BLK, slot)

    @pl.when(b == 0)
    def _prime():
        for cp in a_copies(0) + b_copies(0, 0, 0):
            cp.start()

    for cp in a_copies(b) + b_copies(b, 0, b % 2):
        cp.wait()

    @pl.when(b + 1 < bsz)
    def _prefetch_next():
        for cp in a_copies(b + 1) + b_copies(b + 1, 0, (b + 1) % 2):
            cp.start()

    scale = HEAD_DIM ** -0.5
    rows = 8 * t_new
    ri = lax.broadcasted_iota(jnp.int32, (rows, 1), 0)
    tok = ri // 8
    hd = ri % 8
    slope = jnp.zeros((rows, 1), F32)
    for h in range(N_HEADS):
        slope = jnp.where(hd == h, float(slopes[h]), slope)

    def seg(t, off):
        return zs_ref[t, pl.ds(b, 1), off:off + W_ATT]

    qa = _expand_heads([seg(t, OFF_QA) * scale for t in range(t_new)])
    qb = _expand_heads([seg(t, OFF_QB) * scale for t in range(t_new)])
    upper2 = _upper2(MOBA_BLOCK)

    sa, ys = [], []
    for s in range(t_new):
        sc = jnp.sum(qa * seg(s, OFF_KA), axis=1, keepdims=True)
        sc = sc - slope * (tok - s).astype(F32)
        sa.append(jnp.where(tok >= s, sc, -jnp.inf))
        ys.append(jnp.sum(qb * seg(s, OFF_KB), axis=1, keepdims=True) * LOG2E)
    m = sa[0]
    for s in range(1, t_new):
        m = jnp.maximum(m, sa[s])
    l = jnp.zeros((rows, 1), F32)
    acc = jnp.zeros((rows, W_ATT), F32)
    for s in range(t_new):
        pr = jnp.exp(sa[s] - m)
        l = l + pr
        acc = acc + pr * seg(s, OFF_VA)
    parts = [(m, l, acc)]
    carry = jnp.zeros((rows, 1), F32)
    accb = jnp.zeros((rows, W_ATT), F32)
    for s in range(t_new - 1, -1, -1):
        valid = tok > s
        sp = _softplus2(ys[s])
        a = jnp.where(valid, jnp.exp2(ys[s] - sp + carry), 0.0)
        accb = accb + a * seg(s, OFF_VB)
        carry = carry - jnp.where(valid, sp, 0.0)
    qa_b = qa.astype(BF16)
    qb_b = qb.astype(BF16)
    kcol = lax.broadcasted_iota(jnp.int32, (rows, MOBA_BLOCK), 1)

    gates = []
    for blk in range(n_blk):
        slot = (b % 2) * n_blk + blk
        raw = _dot(qa_b, ka_buf[slot].astype(BF16))
        gates.append(jnp.sum(raw, axis=1, keepdims=True))
        dist = (past_len + tok - blk * MOBA_BLOCK - kcol).astype(F32)
        sc = raw - slope * dist
        m = jnp.max(sc, axis=1, keepdims=True)
        pr = jnp.exp(sc - m)
        parts.append((m, jnp.sum(pr, axis=1, keepdims=True),
                      _dot_nt(pr.astype(BF16), va_buf[slot].astype(BF16))))

    def sb_block(slot, carry, accb):
        y = _dot(qb_b, kb_buf[slot].astype(BF16)) * LOG2E
        sp = _softplus2(y)
        lk = -sp
        aft = _suffix_sums(lk, upper2)
        a = jnp.exp2(y - sp + aft + carry)
        accb = accb + _dot_nt(a.astype(BF16), vb_buf[slot].astype(BF16))
        return carry + aft[:, 0:1] + lk[:, 0:1], accb

    def alive(carry):
        return (jnp.max(carry) > SB_EXIT2).astype(jnp.int32)

    carry, accb = sb_block(b % 2, carry, accb)

    def older_cond(c):
        return (c[0] < n_blk) & (c[1] > 0)

    def older_body(c):
        cps = b_copies(b, c[0], B_SLOTS - 1)
        for cp in cps:
            cp.start()
        for cp in cps:
            cp.wait()
        carry, accb = sb_block(B_SLOTS - 1, c[2], c[3])
        return c[0] + 1, alive(carry), carry, accb

    accb = lax.while_loop(older_cond, older_body, (jnp.int32(1), alive(carry), carry, accb))[3]

    sel = []
    for nb in range(n_blk):
        rank = jnp.zeros((rows, 1), jnp.int32)
        for mth in range(n_blk):
            if mth < nb:
                rank += jnp.where(gates[mth] >= gates[nb], 1, 0)
            elif mth > nb:
                rank += jnp.where(gates[mth] > gates[nb], 1, 0)
        sel.append(rank < MOBA_TOPK)
    big = parts[0][0]
    for nb in range(n_blk):
        big = jnp.maximum(big, jnp.where(sel[nb], parts[nb + 1][0], -jnp.inf))
    w = jnp.exp(parts[0][0] - big)
    l = w * parts[0][1]
    acc = w * parts[0][2]
    for nb in range(n_blk):
        m_nb, l_nb, acc_nb = parts[nb + 1]
        w = jnp.where(sel[nb], jnp.exp(jnp.minimum(m_nb - big, 0.0)), 0.0)
        l = l + w * l_nb
        acc = acc + w * acc_nb
    _reduce_heads(acc / l, oa_ref, b)
    _reduce_heads(accb, ob_ref, b)


def _sample_attn(zs3, caches, layer, page_table, slopes):
    t_new, bsz, _ = zs3.shape
    n_pages = page_table.shape[1]
    n_blk = n_pages // PAGES_PER_BLK
    assert n_pages % PAGES_PER_BLK == 0
    hbm = pl.BlockSpec(memory_space=pl.ANY)
    blk_buf = lambda n: pltpu.VMEM((n, W_ATT, MOBA_BLOCK), F32)
    out_spec = pl.BlockSpec((t_new, bsz, W_ATT), lambda b, pt: (0, 0, 0))
    grid_spec = pltpu.PrefetchScalarGridSpec(
        num_scalar_prefetch=1, grid=(bsz,),
        in_specs=[pl.BlockSpec((t_new, bsz, IN_WIDTH), lambda b, pt: (0, 0, 0)),
                  hbm, hbm, hbm, hbm],
        out_specs=[out_spec, out_spec],
        scratch_shapes=[blk_buf(2 * n_blk), blk_buf(2 * n_blk),
                        blk_buf(B_SLOTS), blk_buf(B_SLOTS),
                        pltpu.SemaphoreType.DMA((2 * n_blk, 2 * PAGES_PER_BLK)),
                        pltpu.SemaphoreType.DMA((B_SLOTS, 2 * PAGES_PER_BLK))])
    body = functools.partial(_sample_attn_body, layer=layer, bsz=bsz, n_pages=n_pages,
                             t_new=t_new, slopes=tuple(float(s) for s in slopes))
    return pl.pallas_call(
        body, grid_spec=grid_spec,
        out_shape=[jax.ShapeDtypeStruct((t_new, bsz, W_ATT), F32)] * 2,
        compiler_params=_params(("arbitrary",)), name="sample_attn")(
            page_table.reshape(-1), zs3, *caches)


def _gate_matrix(kmean_t):
    rows = jnp.arange(W_ATT)[:, None] // HEAD_DIM
    cols = jnp.arange(LANES)[None, :] // 16
    tiled = jnp.tile(kmean_t[:, :, :16], (1, 1, LANES // 16))
    return jnp.swapaxes(jnp.where(rows == cols, tiled, 0.0), 1, 2)


def kernel(x_prompt, x_sample, cache_k_a, cache_v_a, cache_k_b, cache_v_b, page_table, ln_in_g, ln_in_b, w_in, w_out, w_spatial, b_spatial, ln_c_g, ln_c_b, ln1_g, ln1_b, w_ff1, w_ff2, ln2_g, ln2_b):
    batch, seq, _ = x_prompt.shape
    dec_batch, dec_seq, _ = x_sample.shape
    depth = w_in.shape[0]
    alpha = (2 * depth) ** 0.25
    slopes_np = _alibi_slopes(N_HEADS)
    slopes = jnp.asarray(slopes_np)
    n_blk = seq // MOBA_BLOCK
    assert n_blk <= 16 and seq % 512 == 0 and dec_seq <= CHUNK and dec_batch % 8 == 0
    assert all(math.frexp(float(s))[0] == 0.5 for s in slopes_np)

    w_in_b = w_in.astype(BF16)
    w_q = jnp.concatenate([w_in_b[:, :, OFF_QA:OFF_QA + W_ATT], w_in_b[:, :, OFF_QB:OFF_QB + W_ATT],
                           w_in_b[:, :, OFF_UC:]], axis=2)
    w_kv_t = jnp.swapaxes(jnp.concatenate(
        [w_in_b[:, :, OFF_KA:OFF_KA + 2 * W_ATT], w_in_b[:, :, OFF_KB:OFF_KB + 2 * W_ATT]],
        axis=2), 1, 2)
    w_out_b = w_out.astype(BF16)
    w_ff1_b = w_ff1.astype(BF16)
    w_ff2_b = w_ff2.astype(BF16)
    caches = [jnp.transpose(c, (0, 1, 3, 4, 2)).reshape(c.shape[0], c.shape[1], W_ATT, PAGE)
              for c in (cache_k_a, cache_v_a, cache_k_b, cache_v_b)]

    bias_p = jnp.repeat(jnp.swapaxes(b_spatial, 1, 2), HEAD_DIM, axis=2)
    coef_s = jnp.repeat(jnp.transpose(w_spatial[:, :, :dec_seq, :dec_seq], (0, 2, 3, 1)),
                        HEAD_DIM, axis=3).reshape(depth, dec_seq * dec_seq, W_C)
    bias_s = jnp.repeat(jnp.swapaxes(b_spatial[:, :, :dec_seq], 1, 2), HEAD_DIM, axis=2)

    hp = x_prompt.reshape(batch * seq, D_MODEL)
    hs = jnp.swapaxes(x_sample, 0, 1).reshape(dec_seq * dec_batch, D_MODEL)
    kv_p = [jnp.zeros((depth, batch, W_ATT, seq), F32) for _ in range(4)]
    kv_s = [jnp.zeros((depth, dec_seq, W_ATT, dec_batch), F32) for _ in range(4)]
    vc_s = []
    for l in range(depth):
        ln = (ln_in_g, ln_in_b) if l == 0 else None
        lnc_g, lnc_b = ln_c_g[l].reshape(-1), ln_c_b[l].reshape(-1)

        hp, zb, kv_p, kmean_t = _in_proj_prompt(hp, w_q, w_kv_t, ln, kv_p, l, depth, batch, seq)
        selb = _moba_gate(zb, _gate_matrix(kmean_t), batch, seq)
        mix_a = _moba_prompt(zb, kv_p[0], kv_p[1], selb, slopes, l, batch, seq)
        mix_b = _sb_prompt(zb, kv_p[2], kv_p[3], l, batch, seq)
        mix_c = _gmlp_prompt(zb, w_spatial[l], bias_p[l], lnc_g, lnc_b)
        hp = _out_ffn(mix_a, mix_b, mix_c, hp, w_out_b, w_ff1_b, w_ff2_b, l,
                      ln1_g[l], ln1_b[l], ln2_g[l], ln2_b[l], alpha)

        hs, zs, kv_s = _in_proj_sample(hs, w_in_b, w_kv_t, ln, kv_s, l, depth, dec_seq, dec_batch)
        oa, ob = _sample_attn(zs.reshape(dec_seq, dec_batch, IN_WIDTH), caches, l,
                              page_table, slopes_np)
        mix_c, vn = _gmlp_sample(zs, coef_s[l], bias_s[l], lnc_g, lnc_b, dec_seq, dec_batch)
        vc_s.append(vn)
        hs = _out_ffn(oa.reshape(-1, W_ATT), ob.reshape(-1, W_ATT), mix_c, hs,
                      w_out_b, w_ff1_b, w_ff2_b, l,
                      ln1_g[l], ln1_b[l], ln2_g[l], ln2_b[l], alpha)

    kv_out_p = [jnp.transpose(a.reshape(depth, batch, N_HEADS, HEAD_DIM, seq), (0, 1, 4, 2, 3))
                for a in kv_p]
    kv_out_s = [jnp.transpose(a.reshape(depth, dec_seq, N_HEADS, HEAD_DIM, dec_batch), (0, 4, 1, 2, 3))
                for a in kv_s]
    vc_out = jnp.swapaxes(jnp.stack(vc_s).reshape(depth, dec_seq, dec_batch, W_C), 1, 2)
    y_s = jnp.swapaxes(hs.reshape(dec_seq, dec_batch, D_MODEL), 0, 1)
    return (hp.reshape(batch, seq, D_MODEL), y_s, *kv_out_p, *kv_out_s, vc_out)
```

```python
import functools
import math

import jax
import jax.numpy as jnp
import numpy as np
from jax import lax
from jax.experimental import pallas as pl
from jax.experimental.pallas import tpu as pltpu

F32 = jnp.float32
BF16 = jnp.bfloat16

D_MODEL = 1024
HEAD_DIM = 64
N_HEADS = 6
W_ATT = N_HEADS * HEAD_DIM
C_GROUPS = 4
W_C = C_GROUPS * HEAD_DIM
IN_WIDTH = 6 * W_ATT + 2 * W_C
D_FF = 4 * D_MODEL
MOBA_BLOCK = 256
MOBA_TOPK = 3
CHUNK = 128
PAGE = 128
LN_EPS = 1e-5
LANES = 128
NEG_BIG = -1e30
SB_EXIT = -110.0

OFF_QA, OFF_KA, OFF_VA = 0, W_ATT, 2 * W_ATT
OFF_QB, OFF_KB, OFF_VB = 3 * W_ATT, 4 * W_ATT, 5 * W_ATT
OFF_UC, OFF_VC = 6 * W_ATT, 6 * W_ATT + W_C
ZP_QA, ZP_QB, ZP_UC, ZP_VC = 0, W_ATT, 2 * W_ATT, 2 * W_ATT + W_C
ZP_WIDTH = 2 * W_ATT + 2 * W_C

VMEM_LIMIT = 56 * 1024 * 1024


def _alibi_slopes(n):
    def pow2(m):
        start = 2.0 ** (-8.0 / m)
        return [start ** (i + 1) for i in range(m)]
    p = 2 ** int(math.floor(math.log2(n)))
    s = pow2(p)
    if p < n:
        s = s + pow2(2 * p)[0::2][: n - p]
    return np.array(s, dtype=np.float32)


def _ln(x, g, b):
    mu = jnp.mean(x, axis=-1, keepdims=True)
    d = x - mu
    var = jnp.mean(d * d, axis=-1, keepdims=True)
    return d * lax.rsqrt(var + LN_EPS) * g + b


def _dot(a, b, precision=None):
    return jnp.dot(a, b, preferred_element_type=F32, precision=precision)


def _dot_nt(a, b, precision=None):
    return lax.dot_general(a, b, (((1,), (1,)), ((), ())),
                           preferred_element_type=F32, precision=precision)


def _params(sem):
    return pltpu.CompilerParams(dimension_semantics=sem, vmem_limit_bytes=VMEM_LIMIT)


def _in_proj_prompt_body(*refs, apply_ln, tm):
    it = iter(refs)
    x_ref = next(it)
    if apply_ln:
        g_ref, b_ref = next(it), next(it)
    wq_ref, wkv_ref = next(it), next(it)
    for _ in range(4):
        next(it)
    if apply_ln:
        h_ref = next(it)
    zb_ref = next(it)
    kv_refs = [next(it) for _ in range(4)]
    km_ref = next(it)
    ti = pl.program_id(1)

    x = x_ref[...]
    if apply_ln:
        x = _ln(x, g_ref[...], b_ref[...])
        h_ref[...] = x
    hb = x.astype(BF16)
    for c in range(0, ZP_WIDTH, 256):
        zb_ref[:, c:c + 256] = _dot(hb, wq_ref[:, c:c + 256]).astype(BF16)
    lane = lax.broadcasted_iota(jnp.int32, (W_ATT, LANES), 1)

    @pl.when(ti == 0)
    def _zero_means():
        km_ref[...] = jnp.zeros_like(km_ref)

    kt_all = _dot_nt(wkv_ref[...], hb)
    for t in range(4):
        kt = kt_all[t * W_ATT:(t + 1) * W_ATT]
        kv_refs[t][...] = kt
        if t == 0:
            km = km_ref[...]
            for r in range(tm // MOBA_BLOCK):
                mean = jnp.sum(kt[:, r * MOBA_BLOCK:(r + 1) * MOBA_BLOCK], axis=1,
                               keepdims=True) * (1.0 / MOBA_BLOCK)
                km = jnp.where(lane == ti * (tm // MOBA_BLOCK) + r, mean, km)
            km_ref[...] = km


def _in_proj_prompt(x, wq, wkv, ln, kv_prev, layer, depth, batch, seq):
    n = x.shape[0]
    tm = 512
    nt = seq // tm
    row = lambda b, t: (b * nt + t, 0)
    const = lambda b, t: (0, 0)
    args, in_specs = [x], [pl.BlockSpec((tm, D_MODEL), row)]
    if ln is not None:
        args += [ln[0].reshape(1, D_MODEL), ln[1].reshape(1, D_MODEL)]
        in_specs += [pl.BlockSpec((1, D_MODEL), const)] * 2
    args += [wq, wkv]
    in_specs += [pl.BlockSpec((None, D_MODEL, ZP_WIDTH), lambda b, t: (layer, 0, 0)),
                 pl.BlockSpec((None, 4 * W_ATT, D_MODEL), lambda b, t: (layer, 0, 0))]
    out_shape, out_specs = [], []
    if ln is not None:
        out_shape.append(jax.ShapeDtypeStruct((n, D_MODEL), F32))
        out_specs.append(pl.BlockSpec((tm, D_MODEL), row))
    out_shape.append(jax.ShapeDtypeStruct((n, ZP_WIDTH), BF16))
    out_specs.append(pl.BlockSpec((tm, ZP_WIDTH), row))
    aliases = {}
    for t in range(4):
        aliases[len(args)] = len(out_shape)
        args.append(kv_prev[t])
        in_specs.append(pl.BlockSpec(memory_space=pl.ANY))
        out_shape.append(jax.ShapeDtypeStruct((depth, batch, W_ATT, seq), F32))
        out_specs.append(pl.BlockSpec((None, None, W_ATT, tm), lambda b, t: (layer, b, 0, t)))
    out_shape.append(jax.ShapeDtypeStruct((batch, W_ATT, LANES), F32))
    out_specs.append(pl.BlockSpec((None, W_ATT, LANES), lambda b, t: (b, 0, 0)))
    body = functools.partial(_in_proj_prompt_body, apply_ln=ln is not None, tm=tm)
    outs = list(pl.pallas_call(
        body, grid=(batch, nt), in_specs=in_specs, out_specs=out_specs, out_shape=out_shape,
        input_output_aliases=aliases, compiler_params=_params(("parallel", "arbitrary")),
        name="in_proj_prompt")(*args))
    h = outs.pop(0) if ln is not None else x
    return h, outs[0], outs[1:5], outs[5]


def _in_proj_sample_body(*refs, apply_ln, t_new, bsz):
    it = iter(refs)
    x_ref = next(it)
    if apply_ln:
        g_ref, b_ref = next(it), next(it)
    w_ref, wkv_ref = next(it), next(it)
    for _ in range(4):
        next(it)
    if apply_ln:
        h_ref = next(it)
    zs_ref = next(it)
    kv_refs = [next(it) for _ in range(4)]

    x = x_ref[...]
    if apply_ln:
        x = _ln(x, g_ref[...], b_ref[...])
        h_ref[...] = x
    hb = x.astype(BF16)
    for c in range(0, IN_WIDTH, 256):
        zs_ref[:, c:c + 256] = _dot(hb, w_ref[:, c:c + 256])
    for t in range(4):
        kt = _dot_nt(wkv_ref[t * W_ATT:(t + 1) * W_ATT, :], hb)
        for tt in range(t_new):
            kv_refs[t][tt] = kt[:, tt * bsz:(tt + 1) * bsz]


def _in_proj_sample(x, w, wkv, ln, kv_prev, layer, depth, t_new, bsz):
    n = x.shape[0]
    const = lambda i: (0, 0)
    args, in_specs = [x], [pl.BlockSpec((n, D_MODEL), const)]
    if ln is not None:
        args += [ln[0].reshape(1, D_MODEL), ln[1].reshape(1, D_MODEL)]
        in_specs += [pl.BlockSpec((1, D_MODEL), const)] * 2
    args += [w, wkv]
    in_specs += [pl.BlockSpec((None, D_MODEL, IN_WIDTH), lambda i: (layer, 0, 0)),
                 pl.BlockSpec((None, 4 * W_ATT, D_MODEL), lambda i: (layer, 0, 0))]
    out_shape, out_specs = [], []
    if ln is not None:
        out_shape.append(jax.ShapeDtypeStruct((n, D_MODEL), F32))
        out_specs.append(pl.BlockSpec((n, D_MODEL), const))
    out_shape.append(jax.ShapeDtypeStruct((n, IN_WIDTH), F32))
    out_specs.append(pl.BlockSpec((n, IN_WIDTH), const))
    aliases = {}
    for t in range(4):
        aliases[len(args)] = len(out_shape)
        args.append(kv_prev[t])
        in_specs.append(pl.BlockSpec(memory_space=pl.ANY))
        out_shape.append(jax.ShapeDtypeStruct((depth, t_new, W_ATT, bsz), F32))
        out_specs.append(pl.BlockSpec((None, t_new, W_ATT, bsz), lambda i: (layer, 0, 0, 0)))
    body = functools.partial(_in_proj_sample_body, apply_ln=ln is not None, t_new=t_new, bsz=bsz)
    outs = list(pl.pallas_call(
        body, grid=(1,), in_specs=in_specs, out_specs=out_specs, out_shape=out_shape,
        input_output_aliases=aliases, compiler_params=_params(("arbitrary",)),
        name="in_proj_sample")(*args))
    h = outs.pop(0) if ln is not None else x
    return h, outs[0], outs[1:5]


def _gate_body(q_ref, kmt_ref, o_ref):
    cur = pl.program_id(1)
    g = _dot_nt(kmt_ref[...], q_ref[...].astype(F32), precision=lax.Precision.HIGHEST)
    blk = lax.broadcasted_iota(jnp.int32, g.shape, 0) & 15
    past = blk < cur
    g = jnp.where(past, g, -jnp.inf)
    rank = jnp.zeros(g.shape, jnp.int32)
    for s in range(1, 16):
        lower = pltpu.roll(g, s, axis=0)
        rank += jnp.where(lower >= g, jnp.where(blk >= s, 1, 0), 0)
        upper = pltpu.roll(g, LANES - s, axis=0)
        rank += jnp.where(upper > g, jnp.where(blk + s <= 15, 1, 0), 0)
    sel = jnp.where(past, jnp.where(rank < MOBA_TOPK, 0.0, NEG_BIG), NEG_BIG)
    o_ref[...] = sel.T


def _moba_gate(zb, kmt, batch, seq):
    nq = seq // MOBA_BLOCK
    return pl.pallas_call(
        _gate_body, grid=(batch, nq),
        in_specs=[pl.BlockSpec((MOBA_BLOCK, W_ATT), lambda b, i: (b * nq + i, ZP_QA // W_ATT)),
                  pl.BlockSpec((None, LANES, W_ATT), lambda b, i: (b, 0, 0))],
        out_specs=pl.BlockSpec((MOBA_BLOCK, LANES), lambda b, i: (b * nq + i, 0)),
        out_shape=jax.ShapeDtypeStruct((batch * seq, LANES), F32),
        compiler_params=_params(("parallel", "parallel")), name="moba_gate")(zb, kmt)


ROLE_SEL0, ROLE_SEL1 = 0, 1
ROLE_ROW, ROLE_KEY = 96, 97
ROLE_BLK, ROLE_2ND = 98, 99


ATT_PAIRS = 3


def _moba_body(slopes_ref, q_ref, kt_ref, vt_ref, selb_ref, o_ref):
    pg = pl.program_id(1)
    i = pl.program_id(2)
    tq = MOBA_BLOCK
    selb = selb_ref[...]
    lane = lax.broadcasted_iota(jnp.int32, (2 * tq, LANES), 1)
    row2 = lax.broadcasted_iota(jnp.int32, (2 * tq, LANES), 0)
    head1 = row2 >= tq
    rowf = jnp.where(head1, row2 - tq, row2).astype(F32)
    role = lax.broadcasted_iota(jnp.int32, (LANES, 2 * tq), 0)
    key2 = lax.broadcasted_iota(jnp.int32, (LANES, 2 * tq), 1)
    second = key2 >= tq
    keyf = jnp.where(second, key2 - tq, key2).astype(F32)
    rhs_roles = jnp.where(
        (role == ROLE_SEL0) & jnp.logical_not(second), 1.0,
        jnp.where(((role == ROLE_SEL1) | (role == ROLE_2ND)) & second, 1.0,
                  jnp.where((role == ROLE_ROW) | (role == ROLE_BLK), 1.0,
                            jnp.where(role == ROLE_KEY, keyf, 0.0)))).astype(BF16)
    rq = jnp.where(lax.broadcasted_iota(jnp.int32, (2 * tq, tq), 0) >= tq,
                   lax.broadcasted_iota(jnp.int32, (2 * tq, tq), 0) - tq,
                   lax.broadcasted_iota(jnp.int32, (2 * tq, tq), 0))
    ck = lax.broadcasted_iota(jnp.int32, (2 * tq, tq), 1)
    d0 = (rq - ck).astype(F32)
    causal = ck <= rq
    row0 = pl.multiple_of(i * tq, tq)
    fixed = (lane == ROLE_ROW) | (lane == ROLE_KEY) | (lane == ROLE_2ND)

    pairs = []
    for pr_i in range(ATT_PAIRS):
        lanes = slice(pr_i * LANES, (pr_i + 1) * LANES)
        q = q_ref[:, lanes].astype(F32) * (HEAD_DIM ** -0.5)
        q2 = jnp.concatenate([q, q], axis=0)
        qh = jnp.where((lane >= HEAD_DIM) == head1, q2, 0.0).astype(BF16)
        hd0 = 2 * (pg * ATT_PAIRS + pr_i)
        slope = jnp.where(lax.broadcasted_iota(jnp.int32, (2 * tq, 1), 0) >= tq,
                          slopes_ref[hd0 + 1], slopes_ref[hd0])
        lhs_fix = jnp.where(lane == ROLE_ROW, -slope * rowf,
                            jnp.where(lane == ROLE_KEY, slope, slope * tq))
        pairs.append((hd0, slope, lhs_fix, qh))

    def own_block():
        state = []
        for pr_i in range(ATT_PAIRS):
            lanes = slice(pr_i * LANES, (pr_i + 1) * LANES)
            _, slope, _, qh = pairs[pr_i]
            kd = kt_ref[lanes, pl.ds(row0, tq)].astype(BF16)
            vd = vt_ref[lanes, pl.ds(row0, tq)].astype(BF16)
            s = jnp.where(causal, _dot(qh, kd) - slope * d0, -jnp.inf)
            m = jnp.max(s, axis=1, keepdims=True)
            pr = jnp.exp(s - m)
            state += [m, jnp.sum(pr, axis=1, keepdims=True), _dot_nt(pr.astype(BF16), vd)]
        return tuple(state)

    def body(pp, carry):
        j0 = 2 * pp
        r0 = pl.multiple_of(j0 * tq, 2 * tq)
        out = []
        for pr_i in range(ATT_PAIRS):
            lanes = slice(pr_i * LANES, (pr_i + 1) * LANES)
            hd0, slope, lhs_fix, qh = pairs[pr_i]
            m, l, acc = carry[3 * pr_i:3 * pr_i + 3]
            k2 = kt_ref[lanes, pl.ds(r0, 2 * tq)].astype(BF16)
            v2 = vt_ref[lanes, pl.ds(r0, 2 * tq)].astype(BF16)
            sel = jnp.concatenate(
                [pltpu.roll(selb, (LANES - (hd0 + hh) * 16 - j0) % LANES, axis=1) for hh in range(2)],
                axis=0)
            off = -slope * ((i - j0) * tq).astype(F32)
            ext = jnp.where(fixed, lhs_fix, jnp.where(lane == ROLE_BLK, off, sel))
            lhs = jnp.concatenate([qh, ext.astype(BF16)], axis=1)
            rhs = jnp.concatenate([k2, rhs_roles], axis=0)
            t = _dot(lhs, rhs)
            m_new = jnp.maximum(m, jnp.max(t, axis=1, keepdims=True))
            alpha = jnp.exp(m - m_new)
            pr = jnp.exp(t - m_new)
            l = alpha * l + jnp.sum(pr, axis=1, keepdims=True)
            acc = alpha * acc + _dot_nt(pr.astype(BF16), v2)
            out += [m_new, l, acc]
        return tuple(out)

    state = lax.fori_loop(0, (i + 1) // 2, body, own_block())
    for pr_i in range(ATT_PAIRS):
        o = state[3 * pr_i + 2] / state[3 * pr_i + 1]
        o_ref[:, pr_i * LANES:(pr_i + 1) * LANES] = jnp.where(
            lax.broadcasted_iota(jnp.int32, (tq, LANES), 1) < HEAD_DIM, o[:tq], o[tq:]).astype(o_ref.dtype)


def _moba_prompt(zb, kt, vt, selb, slopes, layer, batch, seq):
    nq = seq // MOBA_BLOCK
    wq = ATT_PAIRS * LANES
    cq = ZP_QA // wq
    kv_spec = pl.BlockSpec((None, None, wq, seq), lambda b, p, i, s: (layer, b, p, 0))
    grid_spec = pltpu.PrefetchScalarGridSpec(
        num_scalar_prefetch=1, grid=(batch, N_HEADS // 2 // ATT_PAIRS, nq),
        in_specs=[pl.BlockSpec((MOBA_BLOCK, wq), lambda b, p, i, s: (b * nq + i, cq + p)),
                  kv_spec, kv_spec,
                  pl.BlockSpec((MOBA_BLOCK, LANES), lambda b, p, i, s: (b * nq + i, 0))],
        out_specs=pl.BlockSpec((MOBA_BLOCK, wq), lambda b, p, i, s: (b * nq + i, p)))
    return pl.pallas_call(
        _moba_body, grid_spec=grid_spec,
        out_shape=jax.ShapeDtypeStruct((batch * seq, W_ATT), BF16),
        compiler_params=_params(("parallel", "parallel", "arbitrary")),
        name="moba_prompt")(slopes, zb, kt, vt, selb)


LOG2E = 1.4426950408889634
SB_EXIT2 = SB_EXIT * LOG2E


def _softplus2(y):
    return jnp.maximum(y, 0.0) + jnp.log2(1.0 + jnp.exp2(-jnp.abs(y)))


def _suffix_sums(lk, upper2):
    hi = lk.astype(BF16)
    lo = (lk - hi.astype(F32)).astype(BF16)
    return _dot(jnp.concatenate([hi, lo], axis=1), upper2)


def _upper2(n):
    r = lax.broadcasted_iota(jnp.int32, (2 * n, n), 0)
    c = lax.broadcasted_iota(jnp.int32, (2 * n, n), 1)
    return jnp.where(jnp.where(r >= n, r - n, r) > c, 1.0, 0.0).astype(BF16)


def _sb_body(q_ref, kt_ref, vt_ref, o_ref):
    i = pl.program_id(2)
    tq = MOBA_BLOCK
    lane = lax.broadcasted_iota(jnp.int32, (2 * tq, LANES), 1)
    row2 = lax.broadcasted_iota(jnp.int32, (2 * tq, tq), 0)
    rq = jnp.where(row2 >= tq, row2 - tq, row2)
    ck = lax.broadcasted_iota(jnp.int32, (2 * tq, tq), 1)
    before = ck < rq
    upper2 = _upper2(tq)
    head1 = lax.broadcasted_iota(jnp.int32, (2 * tq, LANES), 0) >= tq
    qhs = []
    for pr_i in range(ATT_PAIRS):
        q = q_ref[:, pr_i * LANES:(pr_i + 1) * LANES].astype(F32) * (HEAD_DIM ** -0.5)
        q2 = jnp.concatenate([q, q], axis=0)
        qhs.append(jnp.where((lane >= HEAD_DIM) == head1, q2, 0.0).astype(BF16))

    def block(j, st, diag):
        r0 = pl.multiple_of(j * tq, tq)
        ys, sps, lks, vbs = [], [], [], []
        for pr_i in range(ATT_PAIRS):
            lanes = slice(pr_i * LANES, (pr_i + 1) * LANES)
            kb = kt_ref[lanes, pl.ds(r0, tq)].astype(BF16)
            vbs.append(vt_ref[lanes, pl.ds(r0, tq)].astype(BF16))
            y = _dot(qhs[pr_i], kb) * LOG2E
            sp = _softplus2(y)
            lk = -sp
            if diag:
                lk = jnp.where(before, lk, 0.0)
            ys.append(y)
            sps.append(sp)
            lks.append(lk)
        aft_all = _suffix_sums(jnp.concatenate(lks, axis=0), upper2)
        out = []
        for pr_i in range(ATT_PAIRS):
            carry, acc = st[2 * pr_i], st[2 * pr_i + 1]
            aft = aft_all[pr_i * 2 * tq:(pr_i + 1) * 2 * tq]
            a = jnp.exp2(ys[pr_i] - sps[pr_i] + aft + carry)
            if diag:
                a = jnp.where(before, a, 0.0)
            out += [carry + aft[:, 0:1] + lks[pr_i][:, 0:1],
                    acc + _dot_nt(a.astype(BF16), vbs[pr_i])]
        return tuple(out)

    def alive(st):
        top = st[0]
        for pr_i in range(1, ATT_PAIRS):
            top = jnp.maximum(top, st[2 * pr_i])
        return (jnp.max(top) > SB_EXIT2).astype(jnp.int32)

    zc, za = jnp.zeros((2 * tq, 1), F32), jnp.zeros((2 * tq, LANES), F32)
    init = (zc, za) * ATT_PAIRS
    st = lax.cond(i >= 1,
                  lambda: block(i - 1, block(i, init, True), False),
                  lambda: block(i, init, True))

    def cond(c):
        return (c[0] >= 0) & (c[1] > 0)

    def body(c):
        st = block(c[0], c[2:], False)
        return (c[0] - 1, alive(st)) + st

    res = lax.while_loop(cond, body, (i - 2, alive(st)) + st)
    for pr_i in range(ATT_PAIRS):
        acc = res[2 + 2 * pr_i + 1]
        o_ref[:, pr_i * LANES:(pr_i + 1) * LANES] = jnp.where(
            lax.broadcasted_iota(jnp.int32, (tq, LANES), 1) < HEAD_DIM, acc[:tq], acc[tq:]).astype(o_ref.dtype)


def _sb_prompt(zb, kt, vt, layer, batch, seq):
    nq = seq // MOBA_BLOCK
    wq = ATT_PAIRS * LANES
    cq = ZP_QB // wq
    kv_spec = pl.BlockSpec((None, None, wq, seq), lambda b, p, i: (layer, b, p, 0))
    return pl.pallas_call(
        _sb_body, grid=(batch, N_HEADS // 2 // ATT_PAIRS, nq),
        in_specs=[pl.BlockSpec((MOBA_BLOCK, wq), lambda b, p, i: (b * nq + i, cq + p)),
                  kv_spec, kv_spec],
        out_specs=pl.BlockSpec((MOBA_BLOCK, wq), lambda b, p, i: (b * nq + i, p)),
        out_shape=jax.ShapeDtypeStruct((batch * seq, W_ATT), BF16),
        compiler_params=_params(("parallel", "parallel", "arbitrary")),
        name="sb_prompt")(zb, kt, vt)


def _group_norm(v, g, b):
    gi = (lax.broadcasted_iota(jnp.int32, (3 * W_C, W_C), 0) % W_C) // HEAD_DIM
    gj = lax.broadcasted_iota(jnp.int32, (3 * W_C, W_C), 1) // HEAD_DIM
    gmean3 = jnp.where(gi == gj, 1.0 / HEAD_DIM, 0.0).astype(BF16)

    def group_mean(x):
        hi = x.astype(BF16)
        r = x - hi.astype(F32)
        mid = r.astype(BF16)
        lo = (r - mid.astype(F32)).astype(BF16)
        return _dot(jnp.concatenate([hi, mid, lo], axis=1), gmean3)

    d = v - group_mean(v)
    var = group_mean(d * d)
    return d * lax.rsqrt(var + LN_EPS) * g + b


def _gmlp_prompt_body(u_ref, v_ref, ws_ref, bias_ref, g_ref, b_ref, o_ref, *, rows):
    ti = lax.broadcasted_iota(jnp.int32, (CHUNK, CHUNK), 0)
    tj = lax.broadcasted_iota(jnp.int32, (CHUNK, CHUNK), 1)
    lane = lax.broadcasted_iota(jnp.int32, (CHUNK, LANES), 1)
    wpair = []
    for pr in range(2):
        w0 = jnp.where(tj <= ti, ws_ref[2 * pr], 0.0)
        w1 = jnp.where(tj <= ti, ws_ref[2 * pr + 1], 0.0)
        wpair.append(jnp.concatenate([w0, w1], axis=1).astype(BF16))
    vn_all = _group_norm(jax.nn.gelu(v_ref[...].astype(F32)), g_ref[...], b_ref[...])
    for r in range(rows // CHUNK):
        sl = slice(r * CHUNK, (r + 1) * CHUNK)
        u = jax.nn.gelu(u_ref[sl, :].astype(F32))
        vn = vn_all[sl, :]
        mixes = []
        for pr in range(2):
            vp = vn[:, pr * LANES:(pr + 1) * LANES]
            rhs = jnp.concatenate([jnp.where(lane < HEAD_DIM, vp, 0.0),
                                   jnp.where(lane >= HEAD_DIM, vp, 0.0)], axis=0).astype(BF16)
            mixes.append(_dot(wpair[pr], rhs))
        mix = jnp.concatenate(mixes, axis=1) + bias_ref[...]
        o_ref[sl, :] = (u * mix).astype(o_ref.dtype)


def _gmlp_prompt(zb, ws, bias, g, b):
    n = zb.shape[0]
    rows = 512
    return pl.pallas_call(
        functools.partial(_gmlp_prompt_body, rows=rows), grid=(n // rows,),
        in_specs=[pl.BlockSpec((rows, W_C), lambda i: (i, ZP_UC // W_C)),
                  pl.BlockSpec((rows, W_C), lambda i: (i, ZP_VC // W_C)),
                  pl.BlockSpec((C_GROUPS, CHUNK, CHUNK), lambda i: (0, 0, 0)),
                  pl.BlockSpec((CHUNK, W_C), lambda i: (0, 0)),
                  pl.BlockSpec((1, W_C), lambda i: (0, 0)),
                  pl.BlockSpec((1, W_C), lambda i: (0, 0))],
        out_specs=pl.BlockSpec((rows, W_C), lambda i: (i, 0)),
        out_shape=jax.ShapeDtypeStruct((n, W_C), BF16),
        compiler_params=_params(("parallel",)), name="gmlp_prompt")(
            zb, zb, ws, bias, g.reshape(1, W_C), b.reshape(1, W_C))


def _gmlp_sample_body(u_ref, v_ref, coef_ref, bias_ref, g_ref, b_ref, o_ref, vn_ref, *, t_new, bsz):
    vns = []
    for t in range(t_new):
        sl = slice(t * bsz, (t + 1) * bsz)
        vn = _group_norm(jax.nn.gelu(v_ref[sl, :]), g_ref[...], b_ref[...])
        vn_ref[sl, :] = vn
        vns.append(vn)
    for t in range(t_new):
        sl = slice(t * bsz, (t + 1) * bsz)
        mix = bias_ref[t:t + 1, :]
        for s in range(t + 1):
            mix = mix + coef_ref[t * t_new + s:t * t_new + s + 1, :] * vns[s]
        o_ref[sl, :] = (jax.nn.gelu(u_ref[sl, :]) * mix).astype(o_ref.dtype)


def _gmlp_sample(zs, coef, bias, g, b, t_new, bsz):
    n = zs.shape[0]
    full = lambda shape: pl.BlockSpec(shape, lambda i: (0, 0))
    return pl.pallas_call(
        functools.partial(_gmlp_sample_body, t_new=t_new, bsz=bsz), grid=(1,),
        in_specs=[pl.BlockSpec((n, W_C), lambda i: (0, OFF_UC // W_C)),
                  pl.BlockSpec((n, W_C), lambda i: (0, OFF_VC // W_C)),
                  full((t_new * t_new, W_C)), full((t_new, W_C)), full((1, W_C)), full((1, W_C))],
        out_specs=[full((n, W_C)), full((n, W_C))],
        out_shape=[jax.ShapeDtypeStruct((n, W_C), BF16), jax.ShapeDtypeStruct((n, W_C), F32)],
        compiler_params=_params(("arbitrary",)), name="gmlp_sample")(
            zs, zs, coef, bias, g.reshape(1, W_C), b.reshape(1, W_C))


def _out_ffn_body(ma_ref, mb_ref, mc_ref, h_ref, wo_ref, g1_ref, b1_ref,
                  w1_ref, w2_ref, g_ref, b_ref, o_ref, acc_ref, *, alpha):
    mixed = jnp.concatenate([ma_ref[...].astype(BF16), mb_ref[...].astype(BF16),
                             mc_ref[...].astype(BF16)], axis=1)
    h = _ln(alpha * h_ref[...] + _dot(mixed, wo_ref[...]), g1_ref[...], b1_ref[...])
    hb = h.astype(BF16)
    step = 512
    for c in range(0, D_FF, step):
        a = _dot(hb, w1_ref[:, c:c + step])
        a = jnp.square(jnp.maximum(a, 0.0)).astype(BF16)
        y = _dot(a, w2_ref[c:c + step, :])
        if c == 0:
            acc_ref[...] = y
        else:
            acc_ref[...] += y
    o_ref[...] = _ln(alpha * h + acc_ref[...], g_ref[...], b_ref[...])


def _out_ffn(ma, mb, mc, h, wo, w1, w2, layer, g1, b1, g2, b2, alpha):
    n = h.shape[0]
    tm = min(512, n)
    row = lambda i: (i, 0)
    const = lambda i: (0, 0)
    lay = lambda i: (layer, 0, 0)
    vec = pl.BlockSpec((1, D_MODEL), const)
    return pl.pallas_call(
        functools.partial(_out_ffn_body, alpha=alpha), grid=(n // tm,),
        in_specs=[pl.BlockSpec((tm, W_ATT), row), pl.BlockSpec((tm, W_ATT), row),
                  pl.BlockSpec((tm, W_C), row), pl.BlockSpec((tm, D_MODEL), row),
                  pl.BlockSpec((None, D_MODEL, D_MODEL), lay, pipeline_mode=pl.Buffered(1)),
                  vec, vec,
                  pl.BlockSpec((None, D_MODEL, D_FF), lay, pipeline_mode=pl.Buffered(1)),
                  pl.BlockSpec((None, D_FF, D_MODEL), lay, pipeline_mode=pl.Buffered(1)),
                  vec, vec],
        out_specs=pl.BlockSpec((tm, D_MODEL), row),
        out_shape=jax.ShapeDtypeStruct((n, D_MODEL), F32),
        scratch_shapes=[pltpu.VMEM((tm, D_MODEL), F32)],
        compiler_params=_params(("parallel",)), name="out_ffn")(
            ma, mb, mc, h, wo, g1.reshape(1, D_MODEL), b1.reshape(1, D_MODEL),
            w1, w2, g2.reshape(1, D_MODEL), b2.reshape(1, D_MODEL))


PAGES_PER_BLK = MOBA_BLOCK // PAGE
B_SLOTS = 3


def _expand_heads(q):
    r = lax.broadcasted_iota(jnp.int32, (8, W_ATT), 0)
    c = lax.broadcasted_iota(jnp.int32, (8, W_ATT), 1) // HEAD_DIM
    pieces = [jnp.where(r == c, jnp.broadcast_to(row, (8, W_ATT)), 0.0) for row in q]
    return jnp.concatenate(pieces, axis=0)


def _reduce_heads(o, out_ref, b):
    r = lax.broadcasted_iota(jnp.int32, (8, W_ATT), 0)
    c = lax.broadcasted_iota(jnp.int32, (8, W_ATT), 1) // HEAD_DIM
    for t in range(o.shape[0] // 8):
        piece = jnp.where(r == c, o[8 * t:8 * t + 8, :], 0.0)
        out_ref[t, pl.ds(b, 1), :] = jnp.sum(piece, axis=0, keepdims=True)


def _sample_attn_body(pt_ref, zs_ref, cka, cva, ckb, cvb, oa_ref, ob_ref,
                      ka_buf, va_buf, kb_buf, vb_buf, sem_a, sem_b,
                      *, layer, bsz, n_pages, t_new, slopes):
    b = pl.program_id(0)
    past_len = n_pages * PAGE
    n_blk = n_pages // PAGES_PER_BLK

    def blk_copies(src_k, src_v, dst_k, dst_v, sem, first_page, slot):
        cps = []
        for k in range(PAGES_PER_BLK):
            pg = pt_ref[first_page + k]
            win = pl.ds(k * PAGE, PAGE)
            cps.append(pltpu.make_async_copy(src_k.at[layer, pg], dst_k.at[slot, :, win],
                                             sem.at[slot, 2 * k]))
            cps.append(pltpu.make_async_copy(src_v.at[layer, pg], dst_v.at[slot, :, win],
                                             sem.at[slot, 2 * k + 1]))
        return cps

    def a_copies(seq):
        cps = []
        for k in range(n_blk):
            cps += blk_copies(cka, cva, ka_buf, va_buf, sem_a,
                              seq * n_pages + k * PAGES_PER_BLK, (seq % 2) * n_blk + k)
        return cps

    def b_copies(seq, back, slot):
        return blk_copies(ckb, cvb, kb_buf, vb_buf, sem_b,
                          seq * n_pages + (n_blk - 1 - back) * PAGES_PER_BLK, slot)

    @pl.when(b == 0)
    def _prime():
        for cp in a_copies(0) + b_copies(0, 0, 0):
            cp.start()

    for cp in a_copies(b) + b_copies(b, 0, b % 2):
        cp.wait()

    @pl.when(b + 1 < bsz)
    def _prefetch_next():
        for cp in a_copies(b + 1) + b_copies(b + 1, 0, (b + 1) % 2):
            cp.start()

    scale = HEAD_DIM ** -0.5
    rows = 8 * t_new
    ri = lax.broadcasted_iota(jnp.int32, (rows, 1), 0)
    tok = ri // 8
    hd = ri % 8
    slope = jnp.zeros((rows, 1), F32)
    for h in range(N_HEADS):
        slope = jnp.where(hd == h, float(slopes[h]), slope)

    def seg(t, off):
        return zs_ref[t, pl.ds(b, 1), off:off + W_ATT]

    qa = _expand_heads([seg(t, OFF_QA) * scale for t in range(t_new)])
    qb = _expand_heads([seg(t, OFF_QB) * scale for t in range(t_new)])
    upper2 = _upper2(MOBA_BLOCK)

    sa, ys = [], []
    for s in range(t_new):
        sc = jnp.sum(qa * seg(s, OFF_KA), axis=1, keepdims=True)
        sc = sc - slope * (tok - s).astype(F32)
        sa.append(jnp.where(tok >= s, sc, -jnp.inf))
        ys.append(jnp.sum(qb * seg(s, OFF_KB), axis=1, keepdims=True) * LOG2E)
    m = sa[0]
    for s in range(1, t_new):
        m = jnp.maximum(m, sa[s])
    l = jnp.zeros((rows, 1), F32)
    acc = jnp.zeros((rows, W_ATT), F32)
    for s in range(t_new):
        pr = jnp.exp(sa[s] - m)
        l = l + pr
        acc = acc + pr * seg(s, OFF_VA)
    parts = [(m, l, acc)]
    carry = jnp.zeros((rows, 1), F32)
    accb = jnp.zeros((rows, W_ATT), F32)
    for s in range(t_new - 1, -1, -1):
        valid = tok > s
        sp = _softplus2(ys[s])
        a = jnp.where(valid, jnp.exp2(ys[s] - sp + carry), 0.0)
        accb = accb + a * seg(s, OFF_VB)
        carry = carry - jnp.where(valid, sp, 0.0)
    qa_b = qa.astype(BF16)
    qb_b = qb.astype(BF16)
    kcol = lax.broadcasted_iota(jnp.int32, (rows, MOBA_BLOCK), 1)

    gates = []
    for blk in range(n_blk):
        slot = (b % 2) * n_blk + blk
        raw = _dot(qa_b, ka_buf[slot].astype(BF16))
        gates.append(jnp.sum(raw, axis=1, keepdims=True))
        dist = (past_len + tok - blk * MOBA_BLOCK - kcol).astype(F32)
        sc = raw - slope * dist
        m = jnp.max(sc, axis=1, keepdims=True)
        pr = jnp.exp(sc - m)
        parts.append((m, jnp.sum(pr, axis=1, keepdims=True),
                      _dot_nt(pr.astype(BF16), va_buf[slot].astype(BF16))))

    def sb_block(slot, carry, accb):
        y = _dot(qb_b, kb_buf[slot].astype(BF16)) * LOG2E
        sp = _softplus2(y)
        lk = -sp
        aft = _suffix_sums(lk, upper2)
        a = jnp.exp2(y - sp + aft + carry)
        accb = accb + _dot_nt(a.astype(BF16), vb_buf[slot].astype(BF16))
        return carry + aft[:, 0:1] + lk[:, 0:1], accb

    def alive(carry):
        return (jnp.max(carry) > SB_EXIT2).astype(jnp.int32)

    carry, accb = sb_block(b % 2, carry, accb)

    def older_cond(c):
        return (c[0] < n_blk) & (c[1] > 0)

    def older_body(c):
        cps = b_copies(b, c[0], B_SLOTS - 1)
        for cp in cps:
            cp.start()
        for cp in cps:
            cp.wait()
        carry, accb = sb_block(B_SLOTS - 1, c[2], c[3])
        return c[0] + 1, alive(carry), carry, accb

    accb = lax.while_loop(older_cond, older_body, (jnp.int32(1), alive(carry), carry, accb))[3]

    sel = []
    for nb in range(n_blk):
        rank = jnp.zeros((rows, 1), jnp.int32)
        for mth in range(n_blk):
            if mth < nb:
                rank += jnp.where(gates[mth] >= gates[nb], 1, 0)
            elif mth > nb:
                rank += jnp.where(gates[mth] > gates[nb], 1, 0)
        sel.append(rank < MOBA_TOPK)
    big = parts[0][0]
    for nb in range(n_blk):
        big = jnp.maximum(big, jnp.where(sel[nb], parts[nb + 1][0], -jnp.inf))
    w = jnp.exp(parts[0][0] - big)
    l = w * parts[0][1]
    acc = w * parts[0][2]
    for nb in range(n_blk):
        m_nb, l_nb, acc_nb = parts[nb + 1]
        w = jnp.where(sel[nb], jnp.exp(jnp.minimum(m_nb - big, 0.0)), 0.0)
        l = l + w * l_nb
        acc = acc + w * acc_nb
    _reduce_heads(acc / l, oa_ref, b)
    _reduce_heads(accb, ob_ref, b)


def _sample_attn(zs3, caches, layer, page_table, slopes):
    t_new, bsz, _ = zs3.shape
    n_pages = page_table.shape[1]
    n_blk = n_pages // PAGES_PER_BLK
    assert n_pages % PAGES_PER_BLK == 0
    hbm = pl.BlockSpec(memory_space=pl.ANY)
    blk_buf = lambda n: pltpu.VMEM((n, W_ATT, MOBA_BLOCK), F32)
    out_spec = pl.BlockSpec((t_new, bsz, W_ATT), lambda b, pt: (0, 0, 0))
    grid_spec = pltpu.PrefetchScalarGridSpec(
        num_scalar_prefetch=1, grid=(bsz,),
        in_specs=[pl.BlockSpec((t_new, bsz, IN_WIDTH), lambda b, pt: (0, 0, 0)),
                  hbm, hbm, hbm, hbm],
        out_specs=[out_spec, out_spec],
        scratch_shapes=[blk_buf(2 * n_blk), blk_buf(2 * n_blk),
                        blk_buf(B_SLOTS), blk_buf(B_SLOTS),
                        pltpu.SemaphoreType.DMA((2 * n_blk, 2 * PAGES_PER_BLK)),
                        pltpu.SemaphoreType.DMA((B_SLOTS, 2 * PAGES_PER_BLK))])
    body = functools.partial(_sample_attn_body, layer=layer, bsz=bsz, n_pages=n_pages,
                             t_new=t_new, slopes=tuple(float(s) for s in slopes))
    return pl.pallas_call(
        body, grid_spec=grid_spec,
        out_shape=[jax.ShapeDtypeStruct((t_new, bsz, W_ATT), F32)] * 2,
        compiler_params=_params(("arbitrary",)), name="sample_attn")(
            page_table.reshape(-1), zs3, *caches)


def _gate_matrix(kmean_t):
    rows = jnp.arange(W_ATT)[:, None] // HEAD_DIM
    cols = jnp.arange(LANES)[None, :] // 16
    tiled = jnp.tile(kmean_t[:, :, :16], (1, 1, LANES // 16))
    return jnp.swapaxes(jnp.where(rows == cols, tiled, 0.0), 1, 2)


def kernel(x_prompt, x_sample, cache_k_a, cache_v_a, cache_k_b, cache_v_b, page_table, ln_in_g, ln_in_b, w_in, w_out, w_spatial, b_spatial, ln_c_g, ln_c_b, ln1_g, ln1_b, w_ff1, w_ff2, ln2_g, ln2_b):
    batch, seq, _ = x_prompt.shape
    dec_batch, dec_seq, _ = x_sample.shape
    depth = w_in.shape[0]
    alpha = (2 * depth) ** 0.25
    slopes_np = _alibi_slopes(N_HEADS)
    slopes = jnp.asarray(slopes_np)
    n_blk = seq // MOBA_BLOCK
    assert n_blk <= 16 and seq % 512 == 0 and dec_seq <= CHUNK and dec_batch % 8 == 0
    assert all(math.frexp(float(s))[0] == 0.5 for s in slopes_np)

    w_in_b = w_in.astype(BF16)
    w_q = jnp.concatenate([w_in_b[:, :, OFF_QA:OFF_QA + W_ATT], w_in_b[:, :, OFF_QB:OFF_QB + W_ATT],
                           w_in_b[:, :, OFF_UC:]], axis=2)
    w_kv_t = jnp.swapaxes(jnp.concatenate(
        [w_in_b[:, :, OFF_KA:OFF_KA + 2 * W_ATT], w_in_b[:, :, OFF_KB:OFF_KB + 2 * W_ATT]],
        axis=2), 1, 2)
    w_out_b = w_out.astype(BF16)
    w_ff1_b = w_ff1.astype(BF16)
    w_ff2_b = w_ff2.astype(BF16)
    caches = [jnp.transpose(c, (0, 1, 3, 4, 2)).reshape(c.shape[0], c.shape[1], W_ATT, PAGE)
              for c in (cache_k_a, cache_v_a, cache_k_b, cache_v_b)]

    bias_p = jnp.repeat(jnp.swapaxes(b_spatial, 1, 2), HEAD_DIM, axis=2)
    coef_s = jnp.repeat(jnp.transpose(w_spatial[:, :, :dec_seq, :dec_seq], (0, 2, 3, 1)),
                        HEAD_DIM, axis=3).reshape(depth, dec_seq * dec_seq, W_C)
    bias_s = jnp.repeat(jnp.swapaxes(b_spatial[:, :, :dec_seq], 1, 2), HEAD_DIM, axis=2)

    hp = x_prompt.reshape(batch * seq, D_MODEL)
    hs = jnp.swapaxes(x_sample, 0, 1).reshape(dec_seq * dec_batch, D_MODEL)
    kv_p = [jnp.zeros((depth, batch, W_ATT, seq), F32) for _ in range(4)]
    kv_s = [jnp.zeros((depth, dec_seq, W_ATT, dec_batch), F32) for _ in range(4)]
    vc_s = []
    for l in range(depth):
        ln = (ln_in_g, ln_in_b) if l == 0 else None
        lnc_g, lnc_b = ln_c_g[l].reshape(-1), ln_c_b[l].reshape(-1)

        hp, zb, kv_p, kmean_t = _in_proj_prompt(hp, w_q, w_kv_t, ln, kv_p, l, depth, batch, seq)
        selb = _moba_gate(zb, _gate_matrix(kmean_t), batch, seq)
        mix_a = _moba_prompt(zb, kv_p[0], kv_p[1], selb, slopes, l, batch, seq)
        mix_b = _sb_prompt(zb, kv_p[2], kv_p[3], l, batch, seq)
        mix_c = _gmlp_prompt(zb, w_spatial[l], bias_p[l], lnc_g, lnc_b)
        hp = _out_ffn(mix_a, mix_b, mix_c, hp, w_out_b, w_ff1_b, w_ff2_b, l,
                      ln1_g[l], ln1_b[l], ln2_g[l], ln2_b[l], alpha)

        hs, zs, kv_s = _in_proj_sample(hs, w_in_b, w_kv_t, ln, kv_s, l, depth, dec_seq, dec_batch)
        oa, ob = _sample_attn(zs.reshape(dec_seq, dec_batch, IN_WIDTH), caches, l,
                              page_table, slopes_np)
        mix_c, vn = _gmlp_sample(zs, coef_s[l], bias_s[l], lnc_g, lnc_b, dec_seq, dec_batch)
        vc_s.append(vn)
        hs = _out_ffn(oa.reshape(-1, W_ATT), ob.reshape(-1, W_ATT), mix_c, hs,
                      w_out_b, w_ff1_b, w_ff2_b, l,
                      ln1_g[l], ln1_b[l], ln2_g[l], ln2_b[l], alpha)

    kv_out_p = [jnp.transpose(a.reshape(depth, batch, N_HEADS, HEAD_DIM, seq), (0, 1, 4, 2, 3))
                for a in kv_p]
    kv_out_s = [jnp.transpose(a.reshape(depth, dec_seq, N_HEADS, HEAD_DIM, dec_batch), (0, 4, 1, 2, 3))
                for a in kv_s]
    vc_out = jnp.swapaxes(jnp.stack(vc_s).reshape(depth, dec_seq, dec_batch, W_C), 1, 2)
    y_s = jnp.swapaxes(hs.reshape(dec_seq, dec_batch, D_MODEL), 0, 1)
    return (hp.reshape(batch, seq, D_MODEL), y_s, *kv_out_p, *kv_out_s, vc_out)
```
